```python
import jax, jax.numpy as jnp
from jax import lax
import numpy as np

D_MODEL = 1024
BATCH = 8
SEQ = 2048
DEPTH = 2

CONV_DIM = 512
CONV_WIDTH = 3
RET_HEADS = 4
RET_HEAD_DIM = 128
RET_DIM = RET_HEADS * RET_HEAD_DIM
GLA_HEADS = 4
GLA_KEY_HEAD_DIM = 64
GLA_VAL_HEAD_DIM = 128
GLA_KEY_DIM = GLA_HEADS * GLA_KEY_HEAD_DIM
GLA_VAL_DIM = GLA_HEADS * GLA_VAL_HEAD_DIM
GLA_GATE_RANK = 16
GLA_GATE_TAU = 16.0
N_BRANCHES = 3
CHUNK = 64
D_FF = (((8 * D_MODEL + 2) // 3 + 255) // 256) * 256
ROPE_BASE = 10000.0
EPS = 1e-6
IN_COLS = 3 * CONV_DIM + 4 * RET_DIM + 2 * GLA_KEY_DIM + 2 * GLA_VAL_DIM + GLA_GATE_RANK + N_BRANCHES * D_MODEL

kernel_name = "hybrid_conv_retention_gla_gated_block"


def rms_norm(x, w):
    xf = x.astype(jnp.float32)
    y = xf * lax.rsqrt(jnp.mean(xf * xf, axis=-1, keepdims=True) + EPS)
    return (y * w.astype(jnp.float32)).astype(x.dtype)


def head_norm(y, w, center):
    b, t, h, d = y.shape
    if center:
        y = y - jnp.mean(y, axis=-1, keepdims=True)
    y = y * lax.rsqrt(jnp.mean(y * y, axis=-1, keepdims=True) + EPS)
    return y.reshape(b, t, h * d) * w.astype(jnp.float32)


def rotary(x, positions):
    half = x.shape[-1] // 2
    inv_freq = ROPE_BASE ** (-jnp.arange(half, dtype=jnp.float32) / half)
    ang = positions.astype(jnp.float32)[..., None] * inv_freq
    cos = jnp.cos(ang)[:, :, None, :]
    sin = jnp.sin(ang)[:, :, None, :]
    x1, x2 = x[..., :half], x[..., half:]
    return jnp.concatenate([x1 * cos - x2 * sin, x2 * cos + x1 * sin], axis=-1)


def to_chunks(x):
    b, t, h, d = x.shape
    return x.reshape(b, t // CHUNK, CHUNK, h, d).transpose(1, 0, 3, 2, 4)


def from_chunks(x):
    nc, b, h, c, d = x.shape
    return x.transpose(1, 0, 3, 2, 4).reshape(b, nc * c, h, d)


def short_conv(u, b_gate, c_gate, conv_w):
    z = c_gate * u
    t = z.shape[1]
    zp = jnp.pad(z, ((0, 0), (CONV_WIDTH - 1, 0), (0, 0)))
    y = sum(conv_w[i] * zp[:, i:i + t, :] for i in range(CONV_WIDTH))
    return b_gate * y


def retention(q, k, v):
    b, t, h, dk = q.shape
    dv = v.shape[-1]
    log_g = jnp.log(1.0 - 2.0 ** (-5.0 - jnp.arange(h, dtype=jnp.float32)))
    idx = jnp.arange(CHUNK, dtype=jnp.float32)
    diff = idx[:, None] - idx[None, :]
    causal = diff >= 0
    decay_mask = jnp.where(causal, jnp.exp(log_g[:, None, None] * jnp.maximum(diff, 0.0)), 0.0)
    q_decay = jnp.exp(log_g[:, None] * (idx + 1.0))[None, :, :, None]
    k_decay = jnp.exp(log_g[:, None] * (CHUNK - 1.0 - idx))[None, :, :, None]
    chunk_decay = jnp.exp(log_g * CHUNK)[None, :, None, None]
    q = q * (dk ** -0.5)

    def step(state, inp):
        qi, ki, vi = inp
        s = jnp.einsum('bhid,bhjd->bhij', qi, ki) * decay_mask
        intra = jnp.einsum('bhij,bhjv->bhiv', s, vi)
        inter = jnp.einsum('bhid,bhdv->bhiv', qi * q_decay, state)
        new_state = state * chunk_decay + jnp.einsum('bhjd,bhjv->bhdv', ki * k_decay, vi)
        return new_state, intra + inter

    state0 = jnp.zeros((b, h, dk, dv), jnp.float32)
    _, out = lax.scan(step, state0, (to_chunks(q), to_chunks(k), to_chunks(v)))
    return from_chunks(out)


def gated_linear_attention(q, k, v, log_a):
    b, t, h, dk = q.shape
    dv = v.shape[-1]
    q = q * (dk ** -0.5)
    tril = jnp.tril(jnp.ones((CHUNK, CHUNK), dtype=bool))[None, None, :, :, None]

    def step(state, inp):
        qi, ki, vi, gi = inp
        cum = jnp.cumsum(gi, axis=2)
        rel = cum[:, :, :, None, :] - cum[:, :, None, :, :]
        pair_decay = jnp.where(tril, jnp.exp(jnp.where(tril, rel, 0.0)), 0.0)
        s = jnp.einsum('bhid,bhjd,bhijd->bhij', qi, ki, pair_decay)
        intra = jnp.einsum('bhij,bhjv->bhiv', s, vi)
        inter = jnp.einsum('bhid,bhdv->bhiv', qi * jnp.exp(cum), state)
        last = cum[:, :, -1:, :]
        new_state = state * jnp.exp(last)[:, :, 0, :, None] + jnp.einsum(
            'bhjd,bhjv->bhdv', ki * jnp.exp(last - cum), vi)
        return new_state, intra + inter

    state0 = jnp.zeros((b, h, dk, dv), jnp.float32)
    _, out = lax.scan(step, state0, (to_chunks(q), to_chunks(k), to_chunks(v), to_chunks(log_a)))
    return from_chunks(out)


def split_columns(proj):
    sizes = ([CONV_DIM] * 3 + [RET_DIM] * 4
             + [GLA_KEY_DIM, GLA_KEY_DIM, GLA_VAL_DIM, GLA_VAL_DIM, GLA_GATE_RANK]
             + [D_MODEL] * N_BRANCHES)
    offsets = [int(o) for o in np.cumsum(sizes)[:-1]]
    return jnp.split(proj, offsets, axis=-1)


def mixer_block(h, positions, w_in, conv_w, ret_gn_w, gla_w_a2, gla_b_a, gla_gn_w,
                w_branch_a, w_branch_b, w_branch_c, w_out):
    bsz, t, _ = h.shape
    proj = (h @ w_in).astype(jnp.float32)
    (cu, cb, cc, rq, rk, rv, rg, gq, gk, gv, gr, ga_down,
     gate_a, gate_b, gate_c) = split_columns(proj)

    y_a = short_conv(cu, cb, cc, conv_w.astype(jnp.float32))

    rq = rotary(rq.reshape(bsz, t, RET_HEADS, RET_HEAD_DIM), positions)
    rk = rotary(rk.reshape(bsz, t, RET_HEADS, RET_HEAD_DIM), positions)
    rv = rv.reshape(bsz, t, RET_HEADS, RET_HEAD_DIM)
    y_b = head_norm(retention(rq, rk, rv), ret_gn_w, center=True) * jax.nn.silu(rg)

    gate_logits = ga_down @ gla_w_a2.astype(jnp.float32) + gla_b_a.astype(jnp.float32)
    log_a = (jax.nn.log_sigmoid(gate_logits) / GLA_GATE_TAU).reshape(bsz, t, GLA_HEADS, GLA_KEY_HEAD_DIM)
    gq = gq.reshape(bsz, t, GLA_HEADS, GLA_KEY_HEAD_DIM)
    gk = gk.reshape(bsz, t, GLA_HEADS, GLA_KEY_HEAD_DIM)
    gv = gv.reshape(bsz, t, GLA_HEADS, GLA_VAL_HEAD_DIM)
    y_c = head_norm(gated_linear_attention(gq, gk, gv, log_a), gla_gn_w, center=False) * jax.nn.silu(gr)

    merged = (jax.nn.sigmoid(gate_a) * (y_a @ w_branch_a.astype(jnp.float32))
              + jax.nn.sigmoid(gate_b) * (y_b @ w_branch_b.astype(jnp.float32))
              + jax.nn.sigmoid(gate_c) * (y_c @ w_branch_c.astype(jnp.float32)))
    return (merged @ w_out.astype(jnp.float32)).astype(h.dtype)


def swiglu(h, w_gate, w_up, w_down):
    return (jax.nn.silu(h @ w_gate) * (h @ w_up)) @ w_down


def setup_inputs(seed: int = 0) -> dict:
    key = jax.random.key(seed)
    ks = jax.random.split(key, 20)
    f32 = jnp.float32

    def normal(k, shape, scale):
        return jax.random.normal(k, shape, f32) * scale

    def gain(k, shape):
        return 1.0 + 0.1 * jax.random.normal(k, shape, f32)

    return {
        "x": normal(ks[0], (BATCH, SEQ, D_MODEL), 1.0),
        "positions": jnp.broadcast_to(jnp.arange(SEQ, dtype=jnp.int32), (BATCH, SEQ)),
        "norm_mix_pre": gain(ks[1], (DEPTH, D_MODEL)),
        "w_in": normal(ks[2], (DEPTH, D_MODEL, IN_COLS), D_MODEL ** -0.5),
        "conv_w": normal(ks[3], (DEPTH, CONV_WIDTH, CONV_DIM), CONV_WIDTH ** -0.5),
        "ret_gn_w": gain(ks[4], (DEPTH, RET_DIM)),
        "gla_w_a2": normal(ks[5], (DEPTH, GLA_GATE_RANK, GLA_KEY_DIM), GLA_GATE_RANK ** -0.5),
        "gla_b_a": normal(ks[6], (DEPTH, GLA_KEY_DIM), 0.1),
        "gla_gn_w": gain(ks[7], (DEPTH, GLA_VAL_DIM)),
        "w_branch_a": normal(ks[8], (DEPTH, CONV_DIM, D_MODEL), CONV_DIM ** -0.5),
        "w_branch_b": normal(ks[9], (DEPTH, RET_DIM, D_MODEL), RET_DIM ** -0.5),
        "w_branch_c": normal(ks[10], (DEPTH, GLA_VAL_DIM, D_MODEL), GLA_VAL_DIM ** -0.5),
        "w_out": normal(ks[11], (DEPTH, D_MODEL, D_MODEL), D_MODEL ** -0.5),
        "norm_mix_post": gain(ks[12], (DEPTH, D_MODEL)),
        "norm_ffn_pre": gain(ks[13], (DEPTH, D_MODEL)),
        "w_ffn_gate": normal(ks[14], (DEPTH, D_MODEL, D_FF), D_MODEL ** -0.5),
        "w_ffn_up": normal(ks[15], (DEPTH, D_MODEL, D_FF), D_MODEL ** -0.5),
        "w_ffn_down": normal(ks[16], (DEPTH, D_FF, D_MODEL), D_FF ** -0.5),
        "norm_ffn_post": gain(ks[17], (DEPTH, D_MODEL)),
    }


def reference(x, positions, norm_mix_pre, w_in, conv_w, ret_gn_w, gla_w_a2, gla_b_a, gla_gn_w,
              w_branch_a, w_branch_b, w_branch_c, w_out, norm_mix_post, norm_ffn_pre,
              w_ffn_gate, w_ffn_up, w_ffn_down, norm_ffn_post):
    for layer in range(DEPTH):
        h = rms_norm(x, norm_mix_pre[layer])
        m = mixer_block(h, positions, w_in[layer], conv_w[layer], ret_gn_w[layer], gla_w_a2[layer],
                        gla_b_a[layer], gla_gn_w[layer], w_branch_a[layer], w_branch_b[layer],
                        w_branch_c[layer], w_out[layer])
        x = x + rms_norm(m, norm_mix_post[layer]).astype(x.dtype)
        h = rms_norm(x, norm_ffn_pre[layer])
        f = swiglu(h, w_ffn_gate[layer], w_ffn_up[layer], w_ffn_down[layer])
        x = x + rms_norm(f, norm_ffn_post[layer]).astype(x.dtype)
    return x
```

```python
import functools
import math

import jax
import jax.numpy as jnp
from jax import lax
from jax.experimental import pallas as pl
from jax.experimental.pallas import tpu as pltpu

D_MODEL = 1024
CONV_DIM = 512
CONV_WIDTH = 3
RET_HEADS = 4
RET_HEAD_DIM = 128
RET_DIM = RET_HEADS * RET_HEAD_DIM
GLA_HEADS = 4
GLA_KEY_HEAD_DIM = 64
GLA_VAL_HEAD_DIM = 128
GLA_KEY_DIM = GLA_HEADS * GLA_KEY_HEAD_DIM
GLA_VAL_DIM = GLA_HEADS * GLA_VAL_HEAD_DIM
GLA_GATE_RANK = 16
GLA_GATE_TAU = 16.0
D_FF = 2816
ROPE_BASE = 10000.0
EPS = 1e-6

OFF_CONV = 0
OFF_RET = 3 * CONV_DIM
OFF_GLA = OFF_RET + 4 * RET_DIM
OFF_GA = OFF_GLA + 2 * GLA_KEY_DIM + 2 * GLA_VAL_DIM
OFF_GATES = OFF_GA + GLA_GATE_RANK

LANES = 128
SUBLANES = 8
GLA_CHUNK = 64
GLA_SUB = 16
N_SUB = GLA_CHUNK // GLA_SUB
RET_CHUNK = 64
VMEM_LIMIT_BYTES = 56 * 1024 * 1024

BF16 = jnp.bfloat16
F32 = jnp.float32


def _dot(a, b):
    return jnp.dot(a, b, preferred_element_type=F32)


def _dot_nt(a, b):
    return lax.dot_general(a, b, (((1,), (1,)), ((), ())), preferred_element_type=F32)


def _dot_tn(a, b):
    return lax.dot_general(a, b, (((0,), (0,)), ((), ())), preferred_element_type=F32)


def _rms(x, w):
    return x * lax.rsqrt(jnp.mean(x * x, axis=-1, keepdims=True) + EPS) * w


def _sigmoid(x):
    return 1.0 / (1.0 + jnp.exp(-x))


def _silu(x):
    return x * _sigmoid(x)


def _log_sigmoid(x):
    return jnp.minimum(x, 0.0) - jnp.log1p(jnp.exp(-jnp.abs(x)))


def _rope_kernel(pos_ref, invf_ref, cos_ref, sin_ref):
    pos = pos_ref[0].astype(F32)
    ang = pos * invf_ref[...]
    lane = lax.broadcasted_iota(jnp.int32, ang.shape, 1)
    sign = jnp.where(lane < RET_HEAD_DIM // 2, -1.0, 1.0)
    cos_ref[0] = jnp.cos(ang)
    sin_ref[0] = jnp.sin(ang) * sign


def _rope_tables(positions, tb):
    b, t = positions.shape
    half = RET_HEAD_DIM // 2
    inv_freq = ROPE_BASE ** (-jnp.arange(half, dtype=F32) / half)
    invf = jnp.concatenate([inv_freq, inv_freq]).reshape(1, RET_HEAD_DIM)
    pos3 = positions.reshape(b, t, 1)
    out = jax.ShapeDtypeStruct((b, t, RET_HEAD_DIM), F32)
    return pl.pallas_call(
        _rope_kernel,
        grid=(b, t // tb),
        in_specs=[pl.BlockSpec((1, tb, 1), lambda i, j: (i, j, 0)),
                  pl.BlockSpec((1, RET_HEAD_DIM), lambda i, j: (0, 0))],
        out_specs=[pl.BlockSpec((1, tb, RET_HEAD_DIM), lambda i, j: (i, j, 0)),
                   pl.BlockSpec((1, tb, RET_HEAD_DIM), lambda i, j: (i, j, 0))],
        out_shape=[out, out],
        name="rope_tables",
    )(pos3, invf)


def _mixer_kernel(x_ref, cos_ref, sin_ref, npre_ref, wmain_ref, wga_ref, wgates_ref, convw_ref,
                  retgn_ref, wa2_ref, ba_ref, glagn_ref, wba_ref, wbb_ref, wbc_ref, wout_ref,
                  npost_ref, emat_ref, out_ref,
                  h_s, p_s, o_s, m_s, zc_s, rstate_s, gstate_s, cum_s, td_s, srep_s, *, tb):
    t_idx = pl.program_id(1)

    @pl.when(t_idx == 0)
    def _():
        zc_s[0:SUBLANES, :] = jnp.zeros((SUBLANES, CONV_DIM), F32)
        rstate_s[...] = jnp.zeros(rstate_s.shape, F32)
        gstate_s[...] = jnp.zeros(gstate_s.shape, F32)

    x = x_ref[0]
    h_s[...] = _rms(x, npre_ref[...]).astype(BF16)
    h = h_s[...]

    p = _dot(h, wmain_ref[:, OFF_CONV:OFF_CONV + 3 * CONV_DIM])
    z = p[:, 2 * CONV_DIM:3 * CONV_DIM] * p[:, 0:CONV_DIM]
    zc_s[SUBLANES:SUBLANES + tb, :] = z
    z1 = zc_s[SUBLANES - 1:SUBLANES - 1 + tb, :]
    z2 = zc_s[SUBLANES - 2:SUBLANES - 2 + tb, :]
    cw = convw_ref[...]
    y_a = p[:, CONV_DIM:2 * CONV_DIM] * (cw[0:1, :] * z2 + cw[1:2, :] * z1 + cw[2:3, :] * z)
    zc_s[0:SUBLANES, :] = z[tb - SUBLANES:tb, :]
    gate = _sigmoid(_dot(h, wgates_ref[:, 0:D_MODEL]))
    m_s[...] = gate * _dot(y_a.astype(BF16), wba_ref[...])

    p_s[:, 0:4 * RET_DIM] = _dot(h, wmain_ref[:, OFF_RET:OFF_RET + 4 * RET_DIM])
    cosf = cos_ref[0]
    sins = sin_ref[0]
    for hh in range(RET_HEADS):
        for base, scale in ((0, RET_HEAD_DIM ** -0.5), (RET_DIM, 1.0)):
            lo = base + hh * RET_HEAD_DIM
            xh = p_s[:, lo:lo + RET_HEAD_DIM]
            r = xh * cosf + pltpu.roll(xh, RET_HEAD_DIM // 2, axis=1) * sins
            p_s[:, lo:lo + RET_HEAD_DIM] = r * scale

    cr = RET_CHUNK
    ri = lax.broadcasted_iota(jnp.int32, (cr, cr), 0)
    ci = lax.broadcasted_iota(jnp.int32, (cr, cr), 1)
    dif = (ri - ci).astype(F32)
    rowf = lax.broadcasted_iota(jnp.int32, (cr, 1), 0).astype(F32)
    ret_consts = []
    for hh in range(RET_HEADS):
        log_g = math.log(1.0 - 2.0 ** (-5.0 - hh))
        dmask = jnp.where(ri >= ci, jnp.exp(log_g * jnp.maximum(dif, 0.0)), 0.0)
        qdec = jnp.exp(log_g * (rowf + 1.0))
        kdec = jnp.exp(log_g * (cr - 1.0 - rowf))
        ret_consts.append((dmask, qdec, kdec, math.exp(log_g * cr)))

    def ret_body(c, carry):
        rows = pl.ds(pl.multiple_of(c * cr, cr), cr)
        for hh in range(RET_HEADS):
            dmask, qdec, kdec, cdec = ret_consts[hh]
            lo = hh * RET_HEAD_DIM
            qc = p_s[rows, lo:lo + RET_HEAD_DIM]
            kc = p_s[rows, RET_DIM + lo:RET_DIM + lo + RET_HEAD_DIM]
            vc = p_s[rows, 2 * RET_DIM + lo:2 * RET_DIM + lo + RET_HEAD_DIM].astype(BF16)
            s = _dot_nt(qc.astype(BF16), kc.astype(BF16)) * dmask
            state = rstate_s[hh]
            o = _dot(s.astype(BF16), vc) + _dot((qc * qdec).astype(BF16), state.astype(BF16))
            rstate_s[hh] = state * cdec + _dot_tn((kc * kdec).astype(BF16), vc)
            o_s[rows, lo:lo + RET_HEAD_DIM] = o
        return carry

    lax.fori_loop(0, tb // cr, ret_body, 0)

    ys = []
    for hh in range(RET_HEADS):
        lo = hh * RET_HEAD_DIM
        oh = o_s[:, lo:lo + RET_HEAD_DIM]
        oc = oh - jnp.mean(oh, axis=-1, keepdims=True)
        ys.append(oc * lax.rsqrt(jnp.mean(oc * oc, axis=-1, keepdims=True) + EPS))
    y_b = jnp.concatenate(ys, axis=1) * retgn_ref[...] * _silu(p_s[:, 3 * RET_DIM:4 * RET_DIM])
    gate = _sigmoid(_dot(h, wgates_ref[:, D_MODEL:2 * D_MODEL]))
    m_s[...] += gate * _dot(y_b.astype(BF16), wbb_ref[...])

    gw = 2 * GLA_KEY_DIM + 2 * GLA_VAL_DIM
    p_s[:, 0:gw] = _dot(h, wmain_ref[:, OFF_GLA:OFF_GLA + gw])
    ga_down = _dot(h, wga_ref[...])
    logits = _dot(ga_down.astype(BF16), wa2_ref[...]) + ba_ref[...]
    cum_s[...] = _log_sigmoid(logits) / GLA_GATE_TAU

    gc = GLA_CHUNK
    kd = GLA_KEY_DIM
    qscale = GLA_KEY_HEAD_DIM ** -0.5
    ri = lax.broadcasted_iota(jnp.int32, (gc, gc), 0)
    ci = lax.broadcasted_iota(jnp.int32, (gc, gc), 1)
    tril = jnp.where(ri >= ci, 1.0, 0.0).astype(BF16)

    def cum_body(c, carry):
        rows = pl.ds(pl.multiple_of(c * gc, gc), gc)
        g = cum_s[rows, :]
        g_hi = g.astype(BF16)
        g_lo = (g - g_hi.astype(F32)).astype(BF16)
        cum = _dot(tril, g_hi) + _dot(tril, g_lo)
        cum_s[rows, :] = cum
        for sb in range(N_SUB):
            r0 = sb * GLA_SUB
            qb = p_s[pl.ds(pl.multiple_of(c * gc + r0, GLA_SUB), GLA_SUB), 0:kd] * qscale
            cb = cum[r0:r0 + GLA_SUB, :]
            for j in range(GLA_SUB):
                kj = p_s[pl.ds(c * gc + r0 + j, 1), kd:2 * kd]
                cj = cum[r0 + j:r0 + j + 1, :]
                tj = qb * kj * jnp.exp(jnp.minimum(cb - cj, 0.0))
                td_s[pl.ds(pl.multiple_of(c * gc + r0, GLA_SUB), GLA_SUB), j * kd:(j + 1) * kd] = tj.astype(BF16)
        return carry

    lax.fori_loop(0, tb // gc, cum_body, 0)
    srep_s[...] = _dot(td_s[...], emat_ref[...])

    rowi = lax.broadcasted_iota(jnp.int32, (gc, LANES), 0)
    lanei = lax.broadcasted_iota(jnp.int32, (gc, LANES), 1)
    colj = lanei & (GLA_KEY_HEAD_DIM - 1)
    sub_shift = GLA_SUB.bit_length() - 1
    diag_mask = ((colj >> sub_shift) == (rowi >> sub_shift)) & (colj <= rowi)
    head_lo = lanei < GLA_KEY_HEAD_DIM
    srow = lax.broadcasted_iota(jnp.int32, (2 * GLA_VAL_HEAD_DIM, LANES), 0)
    slane = lax.broadcasted_iota(jnp.int32, (2 * GLA_VAL_HEAD_DIM, LANES), 1)
    state_mask = (srow < GLA_VAL_HEAD_DIM) == (slane < GLA_KEY_HEAD_DIM)
    zero_v = jnp.zeros((gc, GLA_VAL_HEAD_DIM), BF16)

    def gla_body(c, carry):
        rows = pl.ds(pl.multiple_of(c * gc, gc), gc)
        cum = cum_s[rows, :]
        last = cum[gc - 1:gc, :]
        q = p_s[rows, 0:kd] * qscale
        k = p_s[rows, kd:2 * kd]
        q_in = q * jnp.exp(cum)
        k_st = k * jnp.exp(last - cum)
        a_parts, b_parts = [], []
        rowk = lax.broadcasted_iota(jnp.int32, (gc, kd), 0)
        for jb in range(N_SUB - 1):
            dj = cum[(jb + 1) * GLA_SUB - 1:(jb + 1) * GLA_SUB, :]
            a_parts.append(jnp.where(rowk >= (jb + 1) * GLA_SUB,
                                     q * jnp.exp(jnp.minimum(cum - dj, 0.0)), 0.0))
            b_parts.append(jnp.where((rowk >= jb * GLA_SUB) & (rowk < (jb + 1) * GLA_SUB),
                                     k * jnp.exp(jnp.minimum(dj - cum, 0.0)), 0.0))
        for pr in range(GLA_HEADS // 2):
            ls = slice(pr * LANES, (pr + 1) * LANES)
            a_cat = jnp.concatenate([a[:, ls].astype(BF16) for a in a_parts], axis=1)
            b_lo = jnp.concatenate([jnp.where(head_lo, b[:, ls], 0.0).astype(BF16) for b in b_parts], axis=1)
            b_hi = jnp.concatenate([jnp.where(head_lo, 0.0, b[:, ls]).astype(BF16) for b in b_parts], axis=1)
            b_two = jnp.concatenate([b_lo, b_hi], axis=0)
            s_off = _dot_nt(a_cat, b_two)
            s = jnp.where(diag_mask, srep_s[rows, ls], s_off).astype(BF16)
            v0 = p_s[rows, 2 * kd + (2 * pr) * GLA_VAL_HEAD_DIM:2 * kd + (2 * pr + 1) * GLA_VAL_HEAD_DIM].astype(BF16)
            v1 = p_s[rows, 2 * kd + (2 * pr + 1) * GLA_VAL_HEAD_DIM:2 * kd + (2 * pr + 2) * GLA_VAL_HEAD_DIM].astype(BF16)
            v_bd = jnp.concatenate([jnp.concatenate([v0, zero_v], axis=1),
                                    jnp.concatenate([zero_v, v1], axis=1)], axis=0)
            st = gstate_s[pr]
            o = _dot(s, v_bd) + _dot_nt(q_in[:, ls].astype(BF16), st.astype(BF16))
            v_pair = jnp.concatenate([v0, v1], axis=1)
            upd = _dot_tn(v_pair, k_st[:, ls].astype(BF16))
            gstate_s[pr] = jnp.where(state_mask, st * jnp.exp(last[:, ls]) + upd, 0.0)
            o_s[rows, pr * 2 * GLA_VAL_HEAD_DIM:(pr + 1) * 2 * GLA_VAL_HEAD_DIM] = o
        return carry

    lax.fori_loop(0, tb // gc, gla_body, 0)

    ys = []
    for hh in range(GLA_HEADS):
        lo = hh * GLA_VAL_HEAD_DIM
        oh = o_s[:, lo:lo + GLA_VAL_HEAD_DIM]
        ys.append(oh * lax.rsqrt(jnp.mean(oh * oh, axis=-1, keepdims=True) + EPS))
    y_c = jnp.concatenate(ys, axis=1) * glagn_ref[...] * _silu(p_s[:, 2 * kd + GLA_VAL_DIM:2 * kd + 2 * GLA_VAL_DIM])
    gate = _sigmoid(_dot(h, wgates_ref[:, 2 * D_MODEL:3 * D_MODEL]))
    merged = m_s[...] + gate * _dot(y_c.astype(BF16), wbc_ref[...])

    mixed = _dot(merged.astype(BF16), wout_ref[...])
    out_ref[0] = x + _rms(mixed, npost_ref[...])


def _const_spec(shape):
    nd = len(shape)
    return pl.BlockSpec(shape, lambda i, j: (0,) * nd, pipeline_mode=pl.Buffered(1))


def _diag_reduce_matrix():
    kidx = jnp.arange(GLA_SUB * GLA_KEY_DIM)
    j = kidx // GLA_KEY_DIM
    hk = (kidx % GLA_KEY_DIM) // GLA_KEY_HEAD_DIM
    col = jnp.arange(GLA_KEY_DIM)
    hc = col // GLA_KEY_HEAD_DIM
    cj = (col % GLA_KEY_HEAD_DIM) % GLA_SUB
    return ((hk[:, None] == hc[None, :]) & (j[:, None] == cj[None, :])).astype(BF16)


def _mixer(x, cos_t, sin_t, npre, wmain, wga, wgates, convw, retgn, wa2, ba, glagn, wba, wbb, wbc,
           wout, npost, emat, tb):
    b, t, d = x.shape
    consts = [npre, wmain, wga, wgates, convw, retgn, wa2, ba, glagn, wba, wbb, wbc, wout, npost, emat]
    tok = lambda width: pl.BlockSpec((1, tb, width), lambda i, j: (i, j, 0))
    scratch = [
        pltpu.VMEM((tb, D_MODEL), BF16),
        pltpu.VMEM((tb, 4 * RET_DIM), F32),
        pltpu.VMEM((tb, RET_DIM), F32),
        pltpu.VMEM((tb, D_MODEL), F32),
        pltpu.VMEM((tb + SUBLANES, CONV_DIM), F32),
        pltpu.VMEM((RET_HEADS, RET_HEAD_DIM, RET_HEAD_DIM), F32),
        pltpu.VMEM((GLA_HEADS // 2, 2 * GLA_VAL_HEAD_DIM, LANES), F32),
        pltpu.VMEM((tb, GLA_KEY_DIM), F32),
        pltpu.VMEM((tb, GLA_SUB * GLA_KEY_DIM), BF16),
        pltpu.VMEM((tb, GLA_KEY_DIM), F32),
    ]
    return pl.pallas_call(
        functools.partial(_mixer_kernel, tb=tb),
        grid=(b, t // tb),
        in_specs=[tok(d), tok(RET_HEAD_DIM), tok(RET_HEAD_DIM)] + [_const_spec(c.shape) for c in consts],
        out_specs=tok(d),
        out_shape=jax.ShapeDtypeStruct(x.shape, x.dtype),
        scratch_shapes=scratch,
        compiler_params=pltpu.CompilerParams(dimension_semantics=("arbitrary", "arbitrary"),
                                             vmem_limit_bytes=VMEM_LIMIT_BYTES),
        name="mixer",
    )(x, cos_t, sin_t, *consts)


def _ffn_kernel(x_ref, npre_ref, wg_ref, wu_ref, wd_ref, npost_ref, out_ref, *, fc):
    x = x_ref[...]
    h = _rms(x, npre_ref[...]).astype(BF16)
    acc = jnp.zeros(x.shape, F32)
    for f in range(0, D_FF, fc):
        g = _dot(h, wg_ref[:, f:f + fc])
        u = _dot(h, wu_ref[:, f:f + fc])
        acc = acc + _dot((_silu(g) * u).astype(BF16), wd_ref[f:f + fc, :])
    out_ref[...] = x + _rms(acc, npost_ref[...])


def _ffn(x2, npre, wg, wu, wd, npost, tm, fc):
    n, d = x2.shape
    const = lambda shape: pl.BlockSpec(shape, lambda i: (0,) * len(shape), pipeline_mode=pl.Buffered(1))
    return pl.pallas_call(
        functools.partial(_ffn_kernel, fc=fc),
        grid=(n // tm,),
        in_specs=[pl.BlockSpec((tm, d), lambda i: (i, 0)), const(npre.shape), const(wg.shape),
                  const(wu.shape), const(wd.shape), const(npost.shape)],
        out_specs=pl.BlockSpec((tm, d), lambda i: (i, 0)),
        out_shape=jax.ShapeDtypeStruct(x2.shape, x2.dtype),
        compiler_params=pltpu.CompilerParams(dimension_semantics=("arbitrary",),
                                             vmem_limit_bytes=VMEM_LIMIT_BYTES),
        name="swiglu",
    )(x2, npre, wg, wu, wd, npost)


def _pick_block(t, want):
    tb = min(want, t)
    while t % tb:
        tb //= 2
    return tb


def kernel(x, positions, norm_mix_pre, w_in, conv_w, ret_gn_w, gla_w_a2, gla_b_a, gla_gn_w, w_branch_a, w_branch_b, w_branch_c, w_out, norm_mix_post, norm_ffn_pre, w_ffn_gate, w_ffn_up, w_ffn_down, norm_ffn_post):
    b, t, d = x.shape
    depth = w_in.shape[0]
    tb = _pick_block(t, 256)
    tm = _pick_block(b * t, 512)
    cos_t, sin_t = _rope_tables(positions, _pick_block(t, 512))
    emat = _diag_reduce_matrix()
    row = lambda v: v.reshape(1, -1)
    for l in range(depth):
        wl = w_in[l]
        wmain = wl[:, :OFF_GA].astype(BF16)
        wga = jnp.pad(wl[:, OFF_GA:OFF_GATES], ((0, 0), (0, LANES - GLA_GATE_RANK))).astype(BF16)
        wgates = wl[:, OFF_GATES:].astype(BF16)
        wa2 = jnp.pad(gla_w_a2[l], ((0, LANES - GLA_GATE_RANK), (0, 0))).astype(BF16)
        x = _mixer(x, cos_t, sin_t, row(norm_mix_pre[l]), wmain, wga, wgates, conv_w[l],
                   row(ret_gn_w[l]), wa2, row(gla_b_a[l]), row(gla_gn_w[l]),
                   w_branch_a[l].astype(BF16), w_branch_b[l].astype(BF16), w_branch_c[l].astype(BF16),
                   w_out[l].astype(BF16), row(norm_mix_post[l]), emat, tb)
        x2 = _ffn(x.reshape(b * t, d), row(norm_ffn_pre[l]), w_ffn_gate[l].astype(BF16),
                  w_ffn_up[l].astype(BF16), w_ffn_down[l].astype(BF16), row(norm_ffn_post[l]), tm, 256)
        x = x2.reshape(b, t, d)
    return x
```

```python
import functools
import math

import jax
import jax.numpy as jnp
from jax import lax
from jax.experimental import pallas as pl
from jax.experimental.pallas import tpu as pltpu

D_MODEL = 1024
CONV_DIM = 512
CONV_WIDTH = 3
RET_HEADS = 4
RET_HEAD_DIM = 128
RET_DIM = RET_HEADS * RET_HEAD_DIM
GLA_HEADS = 4
GLA_KEY_HEAD_DIM = 64
GLA_VAL_HEAD_DIM = 128
GLA_KEY_DIM = GLA_HEADS * GLA_KEY_HEAD_DIM
GLA_VAL_DIM = GLA_HEADS * GLA_VAL_HEAD_DIM
GLA_GATE_RANK = 16
GLA_GATE_TAU = 16.0
N_BRANCHES = 3
D_FF = 2816
ROPE_BASE = 10000.0
EPS = 1e-6

OFF_CONV = 0
OFF_RET = 3 * CONV_DIM
OFF_GLA = OFF_RET + 4 * RET_DIM
OFF_GA = OFF_GLA + 2 * GLA_KEY_DIM + 2 * GLA_VAL_DIM
OFF_GATES = OFF_GA + GLA_GATE_RANK
GLA_COLS = 2 * GLA_KEY_DIM + 2 * GLA_VAL_DIM

LANES = 128
SUBLANES = 8
MXU_COLS = 256
GLA_CHUNK = 64
GLA_SUB = 16
N_SUB = GLA_CHUNK // GLA_SUB
VMEM_LIMIT_BYTES = 56 * 1024 * 1024

BF16 = jnp.bfloat16
F32 = jnp.float32


def _dot(a, b):
    return jnp.dot(a, b, preferred_element_type=F32)


def _dot_nt(a, b):
    return lax.dot_general(a, b, (((1,), (1,)), ((), ())), preferred_element_type=F32)


def _dot_tn(a, b):
    return lax.dot_general(a, b, (((0,), (0,)), ((), ())), preferred_element_type=F32)


def _rms(x, w):
    return x * lax.rsqrt(jnp.mean(x * x, axis=-1, keepdims=True) + EPS) * w


def _sigmoid(x):
    return 1.0 / (1.0 + jnp.exp(-x))


def _silu(x):
    return x * _sigmoid(x)


def _log_sigmoid(x):
    return jnp.minimum(x, 0.0) - jnp.log1p(jnp.exp(-jnp.abs(x)))


def _rope_kernel(pos_ref, invf_ref, cos_ref, sin_ref):
    pos = pos_ref[0].astype(F32)
    ang = pos * invf_ref[...]
    lane = lax.broadcasted_iota(jnp.int32, ang.shape, 1)
    sign = jnp.where(lane < RET_HEAD_DIM // 2, -1.0, 1.0)
    cos_ref[0] = jnp.cos(ang)
    sin_ref[0] = jnp.sin(ang) * sign


def _rope_tables(positions, tb):
    b, t = positions.shape
    half = RET_HEAD_DIM // 2
    inv_freq = ROPE_BASE ** (-jnp.arange(half, dtype=F32) / half)
    invf = jnp.concatenate([inv_freq, inv_freq]).reshape(1, RET_HEAD_DIM)
    pos3 = positions.reshape(b, t, 1)
    out = jax.ShapeDtypeStruct((b, t, RET_HEAD_DIM), F32)
    return pl.pallas_call(
        _rope_kernel,
        grid=(b, t // tb),
        in_specs=[pl.BlockSpec((1, tb, 1), lambda i, j: (i, j, 0)),
                  pl.BlockSpec((1, RET_HEAD_DIM), lambda i, j: (0, 0))],
        out_specs=[pl.BlockSpec((1, tb, RET_HEAD_DIM), lambda i, j: (i, j, 0)),
                   pl.BlockSpec((1, tb, RET_HEAD_DIM), lambda i, j: (i, j, 0))],
        out_shape=[out, out],
        name="rope_tables",
    )(pos3, invf)


def _mixer_kernel(x_ref, cos_ref, sin_ref, npre_ref, wmain_ref, wga_ref, wgates_ref, convw_ref,
                  retgn_ref, wa2_ref, ba_ref, glagn_ref, wba_ref, wbb_ref, wbc_ref, wout_ref,
                  npost_ref, emat_ref, out_ref,
                  h_s, pa_s, pb_s, pc_s, gate_s, ob_s, oc_s, m_s, zc_s, rstate_s, gstate_s, cum_s,
                  td_s, srep_s, dmask_s, rdec_s, *, tb):
    b_idx = pl.program_id(0)
    t_idx = pl.program_id(1)

    @pl.when((b_idx == 0) & (t_idx == 0))
    def _():
        ri = lax.broadcasted_iota(jnp.int32, (tb, tb), 0)
        ci = lax.broadcasted_iota(jnp.int32, (tb, tb), 1)
        dif = jnp.maximum(ri - ci, 0).astype(F32)
        rowf = lax.broadcasted_iota(jnp.int32, (tb, RET_HEAD_DIM), 0).astype(F32)
        for hh in range(RET_HEADS):
            log_g = math.log(1.0 - 2.0 ** (-5.0 - hh))
            dmask_s[hh] = jnp.where(ri >= ci, jnp.exp(log_g * dif), 0.0)
            rdec_s[2 * hh] = jnp.exp(log_g * (rowf + 1.0))
            rdec_s[2 * hh + 1] = jnp.exp(log_g * (tb - 1.0 - rowf))

    @pl.when(t_idx == 0)
    def _():
        zc_s[0:SUBLANES, :] = jnp.zeros((SUBLANES, CONV_DIM), F32)
        rstate_s[...] = jnp.zeros(rstate_s.shape, F32)
        gstate_s[...] = jnp.zeros(gstate_s.shape, F32)

    x = x_ref[0]
    h_s[...] = _rms(x, npre_ref[...]).astype(BF16)

    def proj_task(dst_ref, w_ref, w_lo, lo, act=None):
        def run():
            r = _dot(h_s[...], w_ref[:, w_lo + lo:w_lo + lo + MXU_COLS])
            dst_ref[:, lo:lo + MXU_COLS] = r if act is None else act(r)
        return run

    def interleave(primary, fillers):
        done = 0
        for i, step in enumerate(primary):
            step()
            while done * len(primary) < (i + 1) * len(fillers):
                fillers[done]()
                done += 1

    ret_proj = [proj_task(pb_s, wmain_ref, OFF_RET, lo) for lo in range(0, 4 * RET_DIM, MXU_COLS)]
    conv_proj = [proj_task(pa_s, wmain_ref, OFF_CONV, lo) for lo in range(0, 3 * CONV_DIM, MXU_COLS)]
    gate_proj = [proj_task(gate_s, wgates_ref, 0, lo, _sigmoid) for lo in range(0, N_BRANCHES * D_MODEL, MXU_COLS)]

    gc = GLA_CHUNK
    kd = GLA_KEY_DIM
    qscale = GLA_KEY_HEAD_DIM ** -0.5
    n_chunks = tb // gc
    pc_s[...] = _dot(h_s[...], wmain_ref[:, OFF_GLA:OFF_GLA + GLA_COLS])
    ga_down = _dot(h_s[...], wga_ref[...])
    logits = _dot(ga_down.astype(BF16), wa2_ref[...]) + ba_ref[...]
    log_a = _log_sigmoid(logits) / GLA_GATE_TAU
    ri = lax.broadcasted_iota(jnp.int32, (gc, gc), 0)
    ci = lax.broadcasted_iota(jnp.int32, (gc, gc), 1)
    tril = jnp.where(ri >= ci, 1.0, 0.0).astype(BF16)
    for c in range(n_chunks):
        g = log_a[c * gc:(c + 1) * gc, :]
        g_hi = g.astype(BF16)
        g_lo = (g - g_hi.astype(F32)).astype(BF16)
        cum_s[c * gc:(c + 1) * gc, :] = _dot(tril, g_hi) + _dot(tril, g_lo)
    for task in ret_proj[:4]:
        task()

    def diag_unit(u):
        r0 = u * GLA_SUB
        qb = pc_s[r0:r0 + GLA_SUB, 0:kd] * qscale
        cb = cum_s[r0:r0 + GLA_SUB, :]
        for j in range(GLA_SUB):
            kj = pc_s[r0 + j:r0 + j + 1, kd:2 * kd]
            cj = cum_s[r0 + j:r0 + j + 1, :]
            tj = qb * kj * jnp.exp(jnp.minimum(cb - cj, 0.0))
            td_s[r0:r0 + GLA_SUB, j * kd:(j + 1) * kd] = tj.astype(BF16)

    interleave([functools.partial(diag_unit, u) for u in range(tb // GLA_SUB)], ret_proj[4:] + conv_proj)

    def conv_task():
        z = pa_s[:, 2 * CONV_DIM:3 * CONV_DIM] * pa_s[:, 0:CONV_DIM]
        zc_s[SUBLANES:SUBLANES + tb, :] = z
        z1 = zc_s[SUBLANES - 1:SUBLANES - 1 + tb, :]
        z2 = zc_s[SUBLANES - 2:SUBLANES - 2 + tb, :]
        cw = convw_ref[...]
        y_a = pa_s[:, CONV_DIM:2 * CONV_DIM] * (cw[0:1, :] * z2 + cw[1:2, :] * z1 + cw[2:3, :] * z)
        zc_s[0:SUBLANES, :] = z[tb - SUBLANES:tb, :]
        m_s[...] = gate_s[:, 0:D_MODEL] * _dot(y_a.astype(BF16), wba_ref[...])

    ret = [dict() for _ in range(RET_HEADS)]

    def ret_rotary(hh):
        lo = hh * RET_HEAD_DIM
        cosf = cos_ref[0]
        sins = sin_ref[0]
        qh = pb_s[:, lo:lo + RET_HEAD_DIM]
        kh = pb_s[:, RET_DIM + lo:RET_DIM + lo + RET_HEAD_DIM]
        qc = (qh * cosf + pltpu.roll(qh, RET_HEAD_DIM // 2, axis=1) * sins) * (RET_HEAD_DIM ** -0.5)
        kc = kh * cosf + pltpu.roll(kh, RET_HEAD_DIM // 2, axis=1) * sins
        ret[hh].update(q=qc.astype(BF16), k=kc.astype(BF16),
                       qd=(qc * rdec_s[2 * hh]).astype(BF16), kd=(kc * rdec_s[2 * hh + 1]).astype(BF16),
                       v=pb_s[:, 2 * RET_DIM + lo:2 * RET_DIM + lo + RET_HEAD_DIM].astype(BF16))

    def ret_scores(hh):
        r = ret[hh]
        log_g = math.log(1.0 - 2.0 ** (-5.0 - hh))
        state = rstate_s[hh]
        r["s"] = (_dot_nt(r["q"], r["k"]) * dmask_s[hh]).astype(BF16)
        r["inter"] = _dot(r["qd"], state.astype(BF16))
        rstate_s[hh] = state * math.exp(log_g * tb) + _dot_tn(r["kd"], r["v"])

    def ret_output(hh):
        r = ret[hh]
        lo = hh * RET_HEAD_DIM
        o = _dot(r["s"], r["v"]) + r["inter"]
        oc = o - jnp.mean(o, axis=-1, keepdims=True)
        ob_s[:, lo:lo + RET_HEAD_DIM] = oc * lax.rsqrt(jnp.mean(oc * oc, axis=-1, keepdims=True) + EPS)

    rowi = lax.broadcasted_iota(jnp.int32, (gc, LANES), 0)
    lanei = lax.broadcasted_iota(jnp.int32, (gc, LANES), 1)
    colj = lanei & (GLA_KEY_HEAD_DIM - 1)
    sub_shift = GLA_SUB.bit_length() - 1
    diag_mask = ((colj >> sub_shift) == (rowi >> sub_shift)) & (colj <= rowi)
    head_lo = lanei < GLA_KEY_HEAD_DIM
    srow = lax.broadcasted_iota(jnp.int32, (2 * GLA_VAL_HEAD_DIM, LANES), 0)
    slane = lax.broadcasted_iota(jnp.int32, (2 * GLA_VAL_HEAD_DIM, LANES), 1)
    state_mask = (srow < GLA_VAL_HEAD_DIM) == (slane < GLA_KEY_HEAD_DIM)
    zero_v = jnp.zeros((gc, GLA_VAL_HEAD_DIM), BF16)
    rowk = lax.broadcasted_iota(jnp.int32, (gc, kd), 0)

    gla = [[dict() for _ in range(GLA_HEADS // 2)] for _ in range(n_chunks)]

    def gla_intra(c):
        rows = slice(c * gc, (c + 1) * gc)
        cum = cum_s[rows, :]
        last = cum[gc - 1:gc, :]
        q = pc_s[rows, 0:kd] * qscale
        k = pc_s[rows, kd:2 * kd]
        q_in = q * jnp.exp(cum)
        k_st = k * jnp.exp(last - cum)
        a_parts, b_parts = [], []
        for jb in range(N_SUB - 1):
            dj = cum[(jb + 1) * GLA_SUB - 1:(jb + 1) * GLA_SUB, :]
            a_parts.append(jnp.where(rowk >= (jb + 1) * GLA_SUB,
                                     q * jnp.exp(jnp.minimum(cum - dj, 0.0)), 0.0))
            b_parts.append(jnp.where((rowk >= jb * GLA_SUB) & (rowk < (jb + 1) * GLA_SUB),
                                     k * jnp.exp(jnp.minimum(dj - cum, 0.0)), 0.0))
        for pr in range(GLA_HEADS // 2):
            ls = slice(pr * LANES, (pr + 1) * LANES)
            a_cat = jnp.concatenate([a[:, ls].astype(BF16) for a in a_parts], axis=1)
            b_lo = jnp.concatenate([jnp.where(head_lo, b[:, ls], 0.0).astype(BF16) for b in b_parts], axis=1)
            b_hi = jnp.concatenate([jnp.where(head_lo, 0.0, b[:, ls]).astype(BF16) for b in b_parts], axis=1)
            b_two = jnp.concatenate([b_lo, b_hi], axis=0)
            s_off = _dot_nt(a_cat, b_two)
            s = jnp.where(diag_mask, srep_s[rows, ls], s_off).astype(BF16)
            v_lo = 2 * kd + 2 * pr * GLA_VAL_HEAD_DIM
            v0 = pc_s[rows, v_lo:v_lo + GLA_VAL_HEAD_DIM].astype(BF16)
            v1 = pc_s[rows, v_lo + GLA_VAL_HEAD_DIM:v_lo + 2 * GLA_VAL_HEAD_DIM].astype(BF16)
            v_bd = jnp.concatenate([jnp.concatenate([v0, zero_v], axis=1),
                                    jnp.concatenate([zero_v, v1], axis=1)], axis=0)
            v_pair = jnp.concatenate([v0, v1], axis=1)
            upd = _dot_tn(v_pair, k_st[:, ls].astype(BF16))
            gla[c][pr].update(intra=_dot(s, v_bd), q_in=q_in[:, ls].astype(BF16),
                              upd=jnp.where(state_mask, upd, 0.0), decay=jnp.exp(last[:, ls]))

    def gla_carry():
        for pr in range(GLA_HEADS // 2):
            st = gstate_s[pr]
            for c in range(n_chunks):
                g = gla[c][pr]
                o = g["intra"] + _dot_nt(g["q_in"], st.astype(BF16))
                st = st * g["decay"] + g["upd"]
                for hl in range(2):
                    oh = o[:, hl * GLA_VAL_HEAD_DIM:(hl + 1) * GLA_VAL_HEAD_DIM]
                    lo = (2 * pr + hl) * GLA_VAL_HEAD_DIM
                    oc_s[c * gc:(c + 1) * gc, lo:lo + GLA_VAL_HEAD_DIM] = (
                        oh * lax.rsqrt(jnp.mean(oh * oh, axis=-1, keepdims=True) + EPS))
            gstate_s[pr] = st

    def srep_task():
        srep_s[...] = _dot(td_s[...], emat_ref[...])

    heads = range(RET_HEADS)
    interleave([srep_task], gate_proj[:4] + [functools.partial(ret_rotary, hh) for hh in heads])
    interleave([functools.partial(gla_intra, c) for c in range(n_chunks)]
               + [functools.partial(ret_scores, hh) for hh in heads], gate_proj[4:])
    interleave([gla_carry] + [functools.partial(ret_output, hh) for hh in heads], [conv_task])

    y_b = ob_s[...] * retgn_ref[...] * _silu(pb_s[:, 3 * RET_DIM:4 * RET_DIM])
    m_s[...] += gate_s[:, D_MODEL:2 * D_MODEL] * _dot(y_b.astype(BF16), wbb_ref[...])
    y_c = oc_s[...] * glagn_ref[...] * _silu(pc_s[:, 2 * kd + GLA_VAL_DIM:2 * kd + 2 * GLA_VAL_DIM])
    merged = m_s[...] + gate_s[:, 2 * D_MODEL:3 * D_MODEL] * _dot(y_c.astype(BF16), wbc_ref[...])

    mixed = _dot(merged.astype(BF16), wout_ref[...])
    out_ref[0] = x + _rms(mixed, npost_ref[...])


def _const_spec(shape):
    nd = len(shape)
    return pl.BlockSpec(shape, lambda i, j: (0,) * nd, pipeline_mode=pl.Buffered(1))


def _diag_reduce_matrix():
    kidx = jnp.arange(GLA_SUB * GLA_KEY_DIM)
    j = kidx // GLA_KEY_DIM
    hk = (kidx % GLA_KEY_DIM) // GLA_KEY_HEAD_DIM
    col = jnp.arange(GLA_KEY_DIM)
    hc = col // GLA_KEY_HEAD_DIM
    cj = (col % GLA_KEY_HEAD_DIM) % GLA_SUB
    return ((hk[:, None] == hc[None, :]) & (j[:, None] == cj[None, :])).astype(BF16)


def _mixer(x, cos_t, sin_t, npre, wmain, wga, wgates, convw, retgn, wa2, ba, glagn, wba, wbb, wbc,
           wout, npost, emat, tb):
    b, t, d = x.shape
    consts = [npre, wmain, wga, wgates, convw, retgn, wa2, ba, glagn, wba, wbb, wbc, wout, npost, emat]
    tok = lambda width: pl.BlockSpec((1, tb, width), lambda i, j: (i, j, 0))
    scratch = [
        pltpu.VMEM((tb, D_MODEL), BF16),
        pltpu.VMEM((tb, 3 * CONV_DIM), F32),
        pltpu.VMEM((tb, 4 * RET_DIM), F32),
        pltpu.VMEM((tb, GLA_COLS), F32),
        pltpu.VMEM((tb, N_BRANCHES * D_MODEL), F32),
        pltpu.VMEM((tb, RET_DIM), F32),
        pltpu.VMEM((tb, GLA_VAL_DIM), F32),
        pltpu.VMEM((tb, D_MODEL), F32),
        pltpu.VMEM((tb + SUBLANES, CONV_DIM), F32),
        pltpu.VMEM((RET_HEADS, RET_HEAD_DIM, RET_HEAD_DIM), F32),
        pltpu.VMEM((GLA_HEADS // 2, 2 * GLA_VAL_HEAD_DIM, LANES), F32),
        pltpu.VMEM((tb, GLA_KEY_DIM), F32),
        pltpu.VMEM((tb, GLA_SUB * GLA_KEY_DIM), BF16),
        pltpu.VMEM((tb, GLA_KEY_DIM), F32),
        pltpu.VMEM((RET_HEADS, tb, tb), F32),
        pltpu.VMEM((2 * RET_HEADS, tb, RET_HEAD_DIM), F32),
    ]
    return pl.pallas_call(
        functools.partial(_mixer_kernel, tb=tb),
        grid=(b, t // tb),
        in_specs=[tok(d), tok(RET_HEAD_DIM), tok(RET_HEAD_DIM)] + [_const_spec(c.shape) for c in consts],
        out_specs=tok(d),
        out_shape=jax.ShapeDtypeStruct(x.shape, x.dtype),
        scratch_shapes=scratch,
        compiler_params=pltpu.CompilerParams(dimension_semantics=("arbitrary", "arbitrary"),
                                             vmem_limit_bytes=VMEM_LIMIT_BYTES),
        name="mixer",
    )(x, cos_t, sin_t, *consts)


def _ffn_kernel(x_ref, npre_ref, wg_ref, wu_ref, wd_ref, npost_ref, out_ref, *, fc):
    x = x_ref[...]
    h = _rms(x, npre_ref[...]).astype(BF16)
    acc = jnp.zeros(x.shape, F32)
    for f in range(0, D_FF, fc):
        g = _dot(h, wg_ref[:, f:f + fc])
        u = _dot(h, wu_ref[:, f:f + fc])
        acc = acc + _dot((_silu(g) * u).astype(BF16), wd_ref[f:f + fc, :])
    out_ref[...] = x + _rms(acc, npost_ref[...])


def _ffn(x2, npre, wg, wu, wd, npost, tm, fc):
    n, d = x2.shape
    const = lambda shape: pl.BlockSpec(shape, lambda i: (0,) * len(shape), pipeline_mode=pl.Buffered(1))
    return pl.pallas_call(
        functools.partial(_ffn_kernel, fc=fc),
        grid=(n // tm,),
        in_specs=[pl.BlockSpec((tm, d), lambda i: (i, 0)), const(npre.shape), const(wg.shape),
                  const(wu.shape), const(wd.shape), const(npost.shape)],
        out_specs=pl.BlockSpec((tm, d), lambda i: (i, 0)),
        out_shape=jax.ShapeDtypeStruct(x2.shape, x2.dtype),
        compiler_params=pltpu.CompilerParams(dimension_semantics=("arbitrary",),
                                             vmem_limit_bytes=VMEM_LIMIT_BYTES),
        name="swiglu",
    )(x2, npre, wg, wu, wd, npost)


def _pick_block(t, want):
    tb = min(want, t)
    while t % tb:
        tb //= 2
    return tb


def kernel(x, positions, norm_mix_pre, w_in, conv_w, ret_gn_w, gla_w_a2, gla_b_a, gla_gn_w, w_branch_a, w_branch_b, w_branch_c, w_out, norm_mix_post, norm_ffn_pre, w_ffn_gate, w_ffn_up, w_ffn_down, norm_ffn_post):
    b, t, d = x.shape
    depth = w_in.shape[0]
    tb = _pick_block(t, 256)
    tm = _pick_block(b * t, 512)
    cos_t, sin_t = _rope_tables(positions, _pick_block(t, 512))
    emat = _diag_reduce_matrix()
    row = lambda v: v.reshape(1, -1)
    for l in range(depth):
        wl = w_in[l]
        wmain = wl[:, :OFF_GA].astype(BF16)
        wga = jnp.pad(wl[:, OFF_GA:OFF_GATES], ((0, 0), (0, LANES - GLA_GATE_RANK))).astype(BF16)
        wgates = wl[:, OFF_GATES:].astype(BF16)
        wa2 = jnp.pad(gla_w_a2[l], ((0, LANES - GLA_GATE_RANK), (0, 0))).astype(BF16)
        x = _mixer(x, cos_t, sin_t, row(norm_mix_pre[l]), wmain, wga, wgates, conv_w[l],
                   row(ret_gn_w[l]), wa2, row(gla_b_a[l]), row(gla_gn_w[l]),
                   w_branch_a[l].astype(BF16), w_branch_b[l].astype(BF16), w_branch_c[l].astype(BF16),
                   w_out[l].astype(BF16), row(norm_mix_post[l]), emat, tb)
        x2 = _ffn(x.reshape(b * t, d), row(norm_ffn_pre[l]), w_ffn_gate[l].astype(BF16),
                  w_ffn_up[l].astype(BF16), w_ffn_down[l].astype(BF16), row(norm_ffn_post[l]), tm, 256)
        x = x2.reshape(b, t, d)
    return x
```

```python
import functools
import math

import jax
import jax.numpy as jnp
import numpy as np
from jax import lax
from jax.experimental import pallas as pl
from jax.experimental.pallas import tpu as pltpu

D_MODEL = 1024
CONV_DIM = 512
CONV_WIDTH = 3
RET_HEADS = 4
RET_HEAD_DIM = 128
RET_DIM = RET_HEADS * RET_HEAD_DIM
GLA_HEADS = 4
GLA_KEY_HEAD_DIM = 64
GLA_VAL_HEAD_DIM = 128
GLA_KEY_DIM = GLA_HEADS * GLA_KEY_HEAD_DIM
GLA_VAL_DIM = GLA_HEADS * GLA_VAL_HEAD_DIM
GLA_GATE_RANK = 16
GLA_GATE_TAU = 16.0
N_BRANCHES = 3
D_FF = 2816
ROPE_BASE = 10000.0
EPS = 1e-6

OFF_CONV = 0
OFF_RET = 3 * CONV_DIM
OFF_GLA = OFF_RET + 4 * RET_DIM
OFF_GA = OFF_GLA + 2 * GLA_KEY_DIM + 2 * GLA_VAL_DIM
OFF_GATES = OFF_GA + GLA_GATE_RANK
GLA_COLS = 2 * GLA_KEY_DIM + 2 * GLA_VAL_DIM

LANES = 128
W_GATES = OFF_GA
W_GA = W_GATES + N_BRANCHES * D_MODEL
W_COLS = W_GA + LANES
SUBLANES = 8
MXU_COLS = 256
GLA_CHUNK = 64
GLA_SUB = 16
N_SUB = GLA_CHUNK // GLA_SUB
VMEM_LIMIT_BYTES = 56 * 1024 * 1024

BF16 = jnp.bfloat16
F32 = jnp.float32


def _dot(a, b):
    return jnp.dot(a, b, preferred_element_type=F32)


def _dot_nt(a, b):
    return lax.dot_general(a, b, (((1,), (1,)), ((), ())), preferred_element_type=F32)


def _dot_tn(a, b):
    return lax.dot_general(a, b, (((0,), (0,)), ((), ())), preferred_element_type=F32)


def _rms(x, w):
    return x * lax.rsqrt(jnp.mean(x * x, axis=-1, keepdims=True) + EPS) * w


def _sigmoid(x):
    return 1.0 / (1.0 + jnp.exp(-x))


def _silu(x):
    return x * _sigmoid(x)


def _log_sigmoid(x):
    return jnp.minimum(x, 0.0) - jnp.log1p(jnp.exp(-jnp.abs(x)))


def _rope_kernel(pos_ref, invf_ref, cos_ref, sin_ref):
    tb = pos_ref.shape[1]
    hb = tb // 2
    half = RET_HEAD_DIM // 2
    pos = pos_ref[0].astype(F32)
    lo = lax.broadcasted_iota(jnp.int32, (hb, RET_HEAD_DIM), 1) < half
    ang = jnp.where(lo, pos[0:hb], pos[hb:tb]) * invf_ref[...]
    c = jnp.cos(ang)
    s = jnp.sin(ang)
    c_sw = pltpu.roll(c, half, axis=1)
    s_sw = pltpu.roll(s, half, axis=1)
    sign = jnp.where(lo, -1.0, 1.0)
    cos_ref[0, 0:hb, :] = jnp.where(lo, c, c_sw)
    cos_ref[0, hb:tb, :] = jnp.where(lo, c_sw, c)
    sin_ref[0, 0:hb, :] = jnp.where(lo, s, s_sw) * sign
    sin_ref[0, hb:tb, :] = jnp.where(lo, s_sw, s) * sign


def _rope_tables(positions, tb):
    b, t = positions.shape
    half = RET_HEAD_DIM // 2
    inv_freq = ROPE_BASE ** (-jnp.arange(half, dtype=F32) / half)
    invf = jnp.concatenate([inv_freq, inv_freq]).reshape(1, RET_HEAD_DIM)
    pos3 = positions.reshape(b, t, 1)
    out = jax.ShapeDtypeStruct((b, t, RET_HEAD_DIM), F32)
    return pl.pallas_call(
        _rope_kernel,
        grid=(b, t // tb),
        in_specs=[pl.BlockSpec((1, tb, 1), lambda i, j: (i, j, 0)),
                  pl.BlockSpec((1, RET_HEAD_DIM), lambda i, j: (0, 0))],
        out_specs=[pl.BlockSpec((1, tb, RET_HEAD_DIM), lambda i, j: (i, j, 0)),
                   pl.BlockSpec((1, tb, RET_HEAD_DIM), lambda i, j: (i, j, 0))],
        out_shape=[out, out],
        name="rope_tables",
    )(pos3, invf)


def _mixer_kernel(x_ref, cos_ref, sin_ref, npre_ref, win_ref, convw_ref,
                  retgn_ref, wa2_ref, ba_ref, glagn_ref, wba_ref, wbb_ref, wbc_ref, wout_ref,
                  npost_ref, emat_ref, out_ref,
                  h_s, pa_s, pb_s, pc_s, gate_s, ob_s, oc_s, m_s, zc_s, rstate_s, gstate_s, cum_s,
                  td_s, srep_s, dmask_s, rdec_s, *, tb):
    b_idx = pl.program_id(0)
    t_idx = pl.program_id(1)

    @pl.when((b_idx == 0) & (t_idx == 0))
    def _():
        ri = lax.broadcasted_iota(jnp.int32, (tb, tb), 0)
        ci = lax.broadcasted_iota(jnp.int32, (tb, tb), 1)
        dif = jnp.maximum(ri - ci, 0).astype(F32)
        rowf = lax.broadcasted_iota(jnp.int32, (tb, RET_HEAD_DIM), 0).astype(F32)
        for hh in range(RET_HEADS):
            log_g = math.log(1.0 - 2.0 ** (-5.0 - hh))
            dmask_s[hh] = jnp.where(ri >= ci, jnp.exp(log_g * dif), 0.0)
            rdec_s[2 * hh] = jnp.exp(log_g * (rowf + 1.0))
            rdec_s[2 * hh + 1] = jnp.exp(log_g * (tb - 1.0 - rowf))

    @pl.when(t_idx == 0)
    def _():
        zc_s[0:SUBLANES, :] = jnp.zeros((SUBLANES, CONV_DIM), F32)
        rstate_s[...] = jnp.zeros(rstate_s.shape, F32)
        gstate_s[...] = jnp.zeros(gstate_s.shape, F32)

    x = x_ref[0]
    h_s[...] = _rms(x, npre_ref[...]).astype(BF16)

    def proj_task(dst_ref, w_ref, w_lo, lo, act=None):
        def run():
            r = _dot(h_s[...], w_ref[:, w_lo + lo:w_lo + lo + MXU_COLS])
            dst_ref[:, lo:lo + MXU_COLS] = r if act is None else act(r)
        return run

    def interleave(primary, fillers):
        done = 0
        for i, step in enumerate(primary):
            step()
            while done * len(primary) < (i + 1) * len(fillers):
                fillers[done]()
                done += 1

    ret_proj = [proj_task(pb_s, win_ref, OFF_RET, lo) for lo in range(0, 4 * RET_DIM, MXU_COLS)]
    conv_proj = [proj_task(pa_s, win_ref, OFF_CONV, lo) for lo in range(0, 3 * CONV_DIM, MXU_COLS)]
    gate_proj = [proj_task(gate_s, win_ref, W_GATES, lo, _sigmoid) for lo in range(0, N_BRANCHES * D_MODEL, MXU_COLS)]

    gc = GLA_CHUNK
    kd = GLA_KEY_DIM
    qscale = GLA_KEY_HEAD_DIM ** -0.5
    n_chunks = tb // gc
    pc_s[...] = _dot(h_s[...], win_ref[:, OFF_GLA:OFF_GLA + GLA_COLS])
    ga_down = _dot(h_s[...], win_ref[:, W_GA:W_COLS])
    logits = _dot(ga_down.astype(BF16), wa2_ref[...]) + ba_ref[...]
    log_a = _log_sigmoid(logits) / GLA_GATE_TAU
    ri = lax.broadcasted_iota(jnp.int32, (gc, gc), 0)
    ci = lax.broadcasted_iota(jnp.int32, (gc, gc), 1)
    tril = jnp.where(ri >= ci, 1.0, 0.0).astype(BF16)
    for c in range(n_chunks):
        g = log_a[c * gc:(c + 1) * gc, :]
        g_hi = g.astype(BF16)
        g_lo = (g - g_hi.astype(F32)).astype(BF16)
        cum_s[c * gc:(c + 1) * gc, :] = _dot(tril, g_hi) + _dot(tril, g_lo)
    for task in ret_proj[:4]:
        task()

    def diag_unit(u):
        r0 = u * GLA_SUB
        qb = pc_s[r0:r0 + GLA_SUB, 0:kd] * qscale
        cb = cum_s[r0:r0 + GLA_SUB, :]
        for j in range(GLA_SUB):
            kj = pc_s[r0 + j:r0 + j + 1, kd:2 * kd]
            cj = cum_s[r0 + j:r0 + j + 1, :]
            tj = qb * kj * jnp.exp(jnp.minimum(cb - cj, 0.0))
            td_s[r0:r0 + GLA_SUB, j * kd:(j + 1) * kd] = tj.astype(BF16)

    interleave([functools.partial(diag_unit, u) for u in range(tb // GLA_SUB)], ret_proj[4:] + conv_proj)

    def conv_task():
        z = pa_s[:, 2 * CONV_DIM:3 * CONV_DIM] * pa_s[:, 0:CONV_DIM]
        zc_s[SUBLANES:SUBLANES + tb, :] = z
        z1 = zc_s[SUBLANES - 1:SUBLANES - 1 + tb, :]
        z2 = zc_s[SUBLANES - 2:SUBLANES - 2 + tb, :]
        cw = convw_ref[...]
        y_a = pa_s[:, CONV_DIM:2 * CONV_DIM] * (cw[0:1, :] * z2 + cw[1:2, :] * z1 + cw[2:3, :] * z)
        zc_s[0:SUBLANES, :] = z[tb - SUBLANES:tb, :]
        m_s[...] = gate_s[:, 0:D_MODEL] * _dot(y_a.astype(BF16), wba_ref[...])

    ret = [dict() for _ in range(RET_HEADS)]

    def ret_rotary(hh):
        lo = hh * RET_HEAD_DIM
        cosf = cos_ref[0]
        sins = sin_ref[0]
        qh = pb_s[:, lo:lo + RET_HEAD_DIM]
        kh = pb_s[:, RET_DIM + lo:RET_DIM + lo + RET_HEAD_DIM]
        qc = (qh * cosf + pltpu.roll(qh, RET_HEAD_DIM // 2, axis=1) * sins) * (RET_HEAD_DIM ** -0.5)
        kc = kh * cosf + pltpu.roll(kh, RET_HEAD_DIM // 2, axis=1) * sins
        ret[hh].update(q=qc.astype(BF16), k=kc.astype(BF16),
                       qd=(qc * rdec_s[2 * hh]).astype(BF16), kd=(kc * rdec_s[2 * hh + 1]).astype(BF16),
                       v=pb_s[:, 2 * RET_DIM + lo:2 * RET_DIM + lo + RET_HEAD_DIM].astype(BF16))

    def ret_scores(hh):
        r = ret[hh]
        log_g = math.log(1.0 - 2.0 ** (-5.0 - hh))
        state = rstate_s[hh]
        r["s"] = (_dot_nt(r["q"], r["k"]) * dmask_s[hh]).astype(BF16)
        r["inter"] = _dot(r["qd"], state.astype(BF16))
        rstate_s[hh] = state * math.exp(log_g * tb) + _dot_tn(r["kd"], r["v"])

    def ret_output(hh):
        r = ret[hh]
        lo = hh * RET_HEAD_DIM
        o = _dot(r["s"], r["v"]) + r["inter"]
        oc = o - jnp.mean(o, axis=-1, keepdims=True)
        ob_s[:, lo:lo + RET_HEAD_DIM] = oc * lax.rsqrt(jnp.mean(oc * oc, axis=-1, keepdims=True) + EPS)

    rowi = lax.broadcasted_iota(jnp.int32, (gc, LANES), 0)
    lanei = lax.broadcasted_iota(jnp.int32, (gc, LANES), 1)
    colj = lanei & (GLA_KEY_HEAD_DIM - 1)
    sub_shift = GLA_SUB.bit_length() - 1
    diag_mask = ((colj >> sub_shift) == (rowi >> sub_shift)) & (colj <= rowi)
    head_lo = lanei < GLA_KEY_HEAD_DIM
    srow = lax.broadcasted_iota(jnp.int32, (2 * GLA_VAL_HEAD_DIM, LANES), 0)
    slane = lax.broadcasted_iota(jnp.int32, (2 * GLA_VAL_HEAD_DIM, LANES), 1)
    state_mask = (srow < GLA_VAL_HEAD_DIM) == (slane < GLA_KEY_HEAD_DIM)
    zero_v = jnp.zeros((gc, GLA_VAL_HEAD_DIM), BF16)
    rowk = lax.broadcasted_iota(jnp.int32, (gc, kd), 0)

    gla = [[dict() for _ in range(GLA_HEADS // 2)] for _ in range(n_chunks)]

    def gla_intra(c):
        rows = slice(c * gc, (c + 1) * gc)
        cum = cum_s[rows, :]
        last = cum[gc - 1:gc, :]
        q = pc_s[rows, 0:kd] * qscale
        k = pc_s[rows, kd:2 * kd]
        q_in = q * jnp.exp(cum)
        k_st = k * jnp.exp(last - cum)
        a_parts, b_parts = [], []
        for jb in range(N_SUB - 1):
            dj = cum[(jb + 1) * GLA_SUB - 1:(jb + 1) * GLA_SUB, :]
            a_parts.append(jnp.where(rowk >= (jb + 1) * GLA_SUB,
                                     q * jnp.exp(jnp.minimum(cum - dj, 0.0)), 0.0))
            b_parts.append(jnp.where((rowk >= jb * GLA_SUB) & (rowk < (jb + 1) * GLA_SUB),
                                     k * jnp.exp(jnp.minimum(dj - cum, 0.0)), 0.0))
        for pr in range(GLA_HEADS // 2):
            ls = slice(pr * LANES, (pr + 1) * LANES)
            a_cat = jnp.concatenate([a[:, ls].astype(BF16) for a in a_parts], axis=1)
            b_lo = jnp.concatenate([jnp.where(head_lo, b[:, ls], 0.0).astype(BF16) for b in b_parts], axis=1)
            b_hi = jnp.concatenate([jnp.where(head_lo, 0.0, b[:, ls]).astype(BF16) for b in b_parts], axis=1)
            b_two = jnp.concatenate([b_lo, b_hi], axis=0)
            s_off = _dot_nt(a_cat, b_two)
            s = jnp.where(diag_mask, srep_s[rows, ls], s_off).astype(BF16)
            v_lo = 2 * kd + 2 * pr * GLA_VAL_HEAD_DIM
            v0 = pc_s[rows, v_lo:v_lo + GLA_VAL_HEAD_DIM].astype(BF16)
            v1 = pc_s[rows, v_lo + GLA_VAL_HEAD_DIM:v_lo + 2 * GLA_VAL_HEAD_DIM].astype(BF16)
            v_bd = jnp.concatenate([jnp.concatenate([v0, zero_v], axis=1),
                                    jnp.concatenate([zero_v, v1], axis=1)], axis=0)
            v_pair = jnp.concatenate([v0, v1], axis=1)
            upd = _dot_tn(v_pair, k_st[:, ls].astype(BF16))
            gla[c][pr].update(intra=_dot(s, v_bd), q_in=q_in[:, ls].astype(BF16),
                              upd=jnp.where(state_mask, upd, 0.0), decay=jnp.exp(last[:, ls]))

    def gla_carry():
        for pr in range(GLA_HEADS // 2):
            st = gstate_s[pr]
            for c in range(n_chunks):
                g = gla[c][pr]
                o = g["intra"] + _dot_nt(g["q_in"], st.astype(BF16))
                st = st * g["decay"] + g["upd"]
                for hl in range(2):
                    oh = o[:, hl * GLA_VAL_HEAD_DIM:(hl + 1) * GLA_VAL_HEAD_DIM]
                    lo = (2 * pr + hl) * GLA_VAL_HEAD_DIM
                    oc_s[c * gc:(c + 1) * gc, lo:lo + GLA_VAL_HEAD_DIM] = (
                        oh * lax.rsqrt(jnp.mean(oh * oh, axis=-1, keepdims=True) + EPS))
            gstate_s[pr] = st

    def srep_task():
        srep_s[...] = _dot(td_s[...], emat_ref[...])

    heads = range(RET_HEADS)
    interleave([srep_task], gate_proj[:4] + [functools.partial(ret_rotary, hh) for hh in heads])
    interleave([functools.partial(gla_intra, c) for c in range(n_chunks)]
               + [functools.partial(ret_scores, hh) for hh in heads], gate_proj[4:])
    interleave([gla_carry] + [functools.partial(ret_output, hh) for hh in heads], [conv_task])

    y_b = ob_s[...] * retgn_ref[...] * _silu(pb_s[:, 3 * RET_DIM:4 * RET_DIM])
    m_s[...] += gate_s[:, D_MODEL:2 * D_MODEL] * _dot(y_b.astype(BF16), wbb_ref[...])
    y_c = oc_s[...] * glagn_ref[...] * _silu(pc_s[:, 2 * kd + GLA_VAL_DIM:2 * kd + 2 * GLA_VAL_DIM])
    merged = m_s[...] + gate_s[:, 2 * D_MODEL:3 * D_MODEL] * _dot(y_c.astype(BF16), wbc_ref[...])

    mixed = _dot(merged.astype(BF16), wout_ref[...])
    out_ref[0] = x + _rms(mixed, npost_ref[...])


def _layer_spec(arr, layer, grid_rank):
    index = (lambda i: (layer, 0, 0)) if grid_rank == 1 else (lambda i, j: (layer, 0, 0))
    return pl.BlockSpec((None,) + arr.shape[1:], index, pipeline_mode=pl.Buffered(1))


def _diag_reduce_matrix():
    kidx = np.arange(GLA_SUB * GLA_KEY_DIM)
    j = kidx // GLA_KEY_DIM
    hk = (kidx % GLA_KEY_DIM) // GLA_KEY_HEAD_DIM
    col = np.arange(GLA_KEY_DIM)
    hc = col // GLA_KEY_HEAD_DIM
    cj = (col % GLA_KEY_HEAD_DIM) % GLA_SUB
    return jnp.asarray((hk[:, None] == hc[None, :]) & (j[:, None] == cj[None, :]), dtype=BF16)


def _mixer(x, cos_t, sin_t, layer_params, emat, layer, tb):
    b, t, d = x.shape
    tok = lambda width: pl.BlockSpec((1, tb, width), lambda i, j: (i, j, 0))
    const_specs = [_layer_spec(p, layer, 2) for p in layer_params]
    const_specs.append(pl.BlockSpec(emat.shape, lambda i, j: (0, 0), pipeline_mode=pl.Buffered(1)))
    scratch = [
        pltpu.VMEM((tb, D_MODEL), BF16),
        pltpu.VMEM((tb, 3 * CONV_DIM), F32),
        pltpu.VMEM((tb, 4 * RET_DIM), F32),
        pltpu.VMEM((tb, GLA_COLS), F32),
        pltpu.VMEM((tb, N_BRANCHES * D_MODEL), F32),
        pltpu.VMEM((tb, RET_DIM), F32),
        pltpu.VMEM((tb, GLA_VAL_DIM), F32),
        pltpu.VMEM((tb, D_MODEL), F32),
        pltpu.VMEM((tb + SUBLANES, CONV_DIM), F32),
        pltpu.VMEM((RET_HEADS, RET_HEAD_DIM, RET_HEAD_DIM), F32),
        pltpu.VMEM((GLA_HEADS // 2, 2 * GLA_VAL_HEAD_DIM, LANES), F32),
        pltpu.VMEM((tb, GLA_KEY_DIM), F32),
        pltpu.VMEM((tb, GLA_SUB * GLA_KEY_DIM), BF16),
        pltpu.VMEM((tb, GLA_KEY_DIM), F32),
        pltpu.VMEM((RET_HEADS, tb, tb), F32),
        pltpu.VMEM((2 * RET_HEADS, tb, RET_HEAD_DIM), F32),
    ]
    return pl.pallas_call(
        functools.partial(_mixer_kernel, tb=tb),
        grid=(b, t // tb),
        in_specs=[tok(d), tok(RET_HEAD_DIM), tok(RET_HEAD_DIM)] + const_specs,
        out_specs=tok(d),
        out_shape=jax.ShapeDtypeStruct(x.shape, x.dtype),
        scratch_shapes=scratch,
        compiler_params=pltpu.CompilerParams(dimension_semantics=("arbitrary", "arbitrary"),
                                             vmem_limit_bytes=VMEM_LIMIT_BYTES),
        name="mixer",
    )(x, cos_t, sin_t, *layer_params, emat)


def _ffn_kernel(x_ref, npre_ref, wg_ref, wu_ref, wd_ref, npost_ref, out_ref, *, fc):
    x = x_ref[...]
    h = _rms(x, npre_ref[...]).astype(BF16)
    acc = jnp.zeros(x.shape, F32)
    for f in range(0, D_FF, fc):
        g = _dot(h, wg_ref[:, f:f + fc])
        u = _dot(h, wu_ref[:, f:f + fc])
        acc = acc + _dot((_silu(g) * u).astype(BF16), wd_ref[f:f + fc, :])
    out_ref[...] = x + _rms(acc, npost_ref[...])


def _ffn(x2, layer_params, layer, tm, fc):
    n, d = x2.shape
    return pl.pallas_call(
        functools.partial(_ffn_kernel, fc=fc),
        grid=(n // tm,),
        in_specs=[pl.BlockSpec((tm, d), lambda i: (i, 0))] + [_layer_spec(p, layer, 1) for p in layer_params],
        out_specs=pl.BlockSpec((tm, d), lambda i: (i, 0)),
        out_shape=jax.ShapeDtypeStruct(x2.shape, x2.dtype),
        compiler_params=pltpu.CompilerParams(dimension_semantics=("arbitrary",),
                                             vmem_limit_bytes=VMEM_LIMIT_BYTES),
        name="swiglu",
    )(x2, *layer_params)


def _pick_block(t, want):
    tb = min(want, t)
    while t % tb:
        tb //= 2
    return tb


def kernel(x, positions, norm_mix_pre, w_in, conv_w, ret_gn_w, gla_w_a2, gla_b_a, gla_gn_w, w_branch_a, w_branch_b, w_branch_c, w_out, norm_mix_post, norm_ffn_pre, w_ffn_gate, w_ffn_up, w_ffn_down, norm_ffn_post):
    b, t, d = x.shape
    depth = w_in.shape[0]
    tb = _pick_block(t, 256)
    tm = _pick_block(b * t, 512)
    cos_t, sin_t = _rope_tables(positions, _pick_block(t, 512))
    emat = _diag_reduce_matrix()
    rows = lambda v: v.reshape(depth, 1, -1)
    pad_rank = LANES - GLA_GATE_RANK
    w_packed = jnp.concatenate(
        [w_in[:, :, :OFF_GA], w_in[:, :, OFF_GATES:],
         jnp.pad(w_in[:, :, OFF_GA:OFF_GATES], ((0, 0), (0, 0), (0, pad_rank)))], axis=2).astype(BF16)
    wa2 = jnp.pad(gla_w_a2, ((0, 0), (0, pad_rank), (0, 0))).astype(BF16)
    mixer_params = [rows(norm_mix_pre), w_packed, conv_w, rows(ret_gn_w), wa2, rows(gla_b_a), rows(gla_gn_w),
                    w_branch_a.astype(BF16), w_branch_b.astype(BF16), w_branch_c.astype(BF16),
                    w_out.astype(BF16), rows(norm_mix_post)]
    ffn_params = [rows(norm_ffn_pre), w_ffn_gate.astype(BF16), w_ffn_up.astype(BF16),
                  w_ffn_down.astype(BF16), rows(norm_ffn_post)]
    for l in range(depth):
        x = _mixer(x, cos_t, sin_t, mixer_params, emat, l, tb)
        x = _ffn(x.reshape(b * t, d), ffn_params, l, tm, 256).reshape(b, t, d)
    return x
```

```python
import functools
import math

import jax
import jax.numpy as jnp
import numpy as np
from jax import lax
from jax.experimental import pallas as pl
from jax.experimental.pallas import tpu as pltpu

D_MODEL = 1024
CONV_DIM = 512
CONV_WIDTH = 3
RET_HEADS = 4
RET_HEAD_DIM = 128
RET_DIM = RET_HEADS * RET_HEAD_DIM
GLA_HEADS = 4
GLA_KEY_HEAD_DIM = 64
GLA_VAL_HEAD_DIM = 128
GLA_KEY_DIM = GLA_HEADS * GLA_KEY_HEAD_DIM
GLA_VAL_DIM = GLA_HEADS * GLA_VAL_HEAD_DIM
GLA_GATE_RANK = 16
GLA_GATE_TAU = 16.0
N_BRANCHES = 3
D_FF = 2816
ROPE_BASE = 10000.0
EPS = 1e-6
LOG2E = 1.0 / math.log(2.0)

OFF_CONV = 0
OFF_RET = 3 * CONV_DIM
OFF_GLA = OFF_RET + 4 * RET_DIM
OFF_GA = OFF_GLA + 2 * GLA_KEY_DIM + 2 * GLA_VAL_DIM
OFF_GATES = OFF_GA + GLA_GATE_RANK
GLA_COLS = 2 * GLA_KEY_DIM + 2 * GLA_VAL_DIM

LANES = 128
W_GATES = OFF_GA
W_GA = W_GATES + N_BRANCHES * D_MODEL
W_COLS = W_GA + LANES
SUBLANES = 8
MXU_COLS = 256
GLA_CHUNK = 64
GLA_SUB = 16
N_SUB = GLA_CHUNK // GLA_SUB
VMEM_LIMIT_BYTES = 56 * 1024 * 1024

BF16 = jnp.bfloat16
F32 = jnp.float32


def _dot(a, b):
    return jnp.dot(a, b, preferred_element_type=F32)


def _dot_nt(a, b):
    return lax.dot_general(a, b, (((1,), (1,)), ((), ())), preferred_element_type=F32)


def _dot_tn(a, b):
    return lax.dot_general(a, b, (((0,), (0,)), ((), ())), preferred_element_type=F32)


def _rms(x, w):
    return x * lax.rsqrt(jnp.mean(x * x, axis=-1, keepdims=True) + EPS) * w


def _sigmoid(x):
    return 1.0 / (1.0 + jnp.exp2(x * -LOG2E))


def _silu(x):
    return x * _sigmoid(x)


def _log_sigmoid(x):
    return jnp.minimum(x, 0.0) - jnp.log(1.0 + jnp.exp2(jnp.abs(x) * -LOG2E))


def _rope_kernel(pos_ref, invf_ref, cos_ref, sin_ref):
    tb = pos_ref.shape[1]
    hb = tb // 2
    half = RET_HEAD_DIM // 2
    pos = pos_ref[0].astype(F32)
    lo = lax.broadcasted_iota(jnp.int32, (hb, RET_HEAD_DIM), 1) < half
    ang = jnp.where(lo, pos[0:hb], pos[hb:tb]) * invf_ref[...]
    c = jnp.cos(ang)
    s = jnp.sin(ang)
    c_sw = pltpu.roll(c, half, axis=1)
    s_sw = pltpu.roll(s, half, axis=1)
    sign = jnp.where(lo, -1.0, 1.0)
    cos_ref[0, 0:hb, :] = jnp.where(lo, c, c_sw)
    cos_ref[0, hb:tb, :] = jnp.where(lo, c_sw, c)
    sin_ref[0, 0:hb, :] = jnp.where(lo, s, s_sw) * sign
    sin_ref[0, hb:tb, :] = jnp.where(lo, s_sw, s) * sign


def _rope_tables(positions, tb):
    b, t = positions.shape
    half = RET_HEAD_DIM // 2
    inv_freq = ROPE_BASE ** (-jnp.arange(half, dtype=F32) / half)
    invf = jnp.concatenate([inv_freq, inv_freq]).reshape(1, RET_HEAD_DIM)
    pos3 = positions.reshape(b, t, 1)
    out = jax.ShapeDtypeStruct((b, t, RET_HEAD_DIM), F32)
    return pl.pallas_call(
        _rope_kernel,
        grid=(b, t // tb),
        in_specs=[pl.BlockSpec((1, tb, 1), lambda i, j: (i, j, 0)),
                  pl.BlockSpec((1, RET_HEAD_DIM), lambda i, j: (0, 0))],
        out_specs=[pl.BlockSpec((1, tb, RET_HEAD_DIM), lambda i, j: (i, j, 0)),
                   pl.BlockSpec((1, tb, RET_HEAD_DIM), lambda i, j: (i, j, 0))],
        out_shape=[out, out],
        name="rope_tables",
    )(pos3, invf)


def _mixer_kernel(x_ref, xnext_ref, cos_ref, sin_ref, npre_ref, win_ref, convw_ref,
                  retgn_ref, wa2_ref, ba_ref, glagn_ref, wba_ref, wbb_ref, wbc_ref, wout_ref,
                  npost_ref, emat_ref, out_ref,
                  h_s, hn_s, pa_s, pb_s, pc_s, gate_s, ob_s, oc_s, m_s, zc_s, rstate_s, gstate_s, cum_s,
                  td_s, srep_s, dmask_s, rdec_s, *, tb):
    b_idx = pl.program_id(0)
    t_idx = pl.program_id(1)

    @pl.when((b_idx == 0) & (t_idx == 0))
    def _():
        ri = lax.broadcasted_iota(jnp.int32, (tb, tb), 0)
        ci = lax.broadcasted_iota(jnp.int32, (tb, tb), 1)
        dif = jnp.maximum(ri - ci, 0).astype(F32)
        rowf = lax.broadcasted_iota(jnp.int32, (tb, RET_HEAD_DIM), 0).astype(F32)
        for hh in range(RET_HEADS):
            log_g = math.log(1.0 - 2.0 ** (-5.0 - hh))
            dmask_s[hh] = jnp.where(ri >= ci, jnp.exp(log_g * dif), 0.0)
            rdec_s[2 * hh] = jnp.exp(log_g * (rowf + 1.0))
            rdec_s[2 * hh + 1] = jnp.exp(log_g * (tb - 1.0 - rowf))

    @pl.when(t_idx == 0)
    def _():
        zc_s[0:SUBLANES, :] = jnp.zeros((SUBLANES, CONV_DIM), F32)
        rstate_s[...] = jnp.zeros(rstate_s.shape, F32)
        gstate_s[...] = jnp.zeros(gstate_s.shape, F32)

    @pl.when((b_idx == 0) & (t_idx == 0))
    def _():
        h_s[...] = _rms(x_ref[0], npre_ref[...]).astype(BF16)

    half_tb = tb // 2

    def next_h_task(r0):
        def run():
            xn = xnext_ref[0, r0:r0 + half_tb, :]
            hn_s[r0:r0 + half_tb, :] = _rms(xn, npre_ref[...]).astype(BF16)
        return run

    def proj_task(dst_ref, w_ref, w_lo, lo, act=None):
        def run():
            r = _dot(h_s[...], w_ref[:, w_lo + lo:w_lo + lo + MXU_COLS])
            dst_ref[:, lo:lo + MXU_COLS] = r if act is None else act(r)
        return run

    def interleave(primary, fillers):
        done = 0
        for i, step in enumerate(primary):
            step()
            while done * len(primary) < (i + 1) * len(fillers):
                fillers[done]()
                done += 1

    ret_proj = [proj_task(pb_s, win_ref, OFF_RET, lo) for lo in range(0, 4 * RET_DIM, MXU_COLS)]
    conv_proj = [proj_task(pa_s, win_ref, OFF_CONV, lo) for lo in range(0, 3 * CONV_DIM, MXU_COLS)]
    gate_proj = [proj_task(gate_s, win_ref, W_GATES, lo, _sigmoid) for lo in range(0, N_BRANCHES * D_MODEL, MXU_COLS)]

    gc = GLA_CHUNK
    kd = GLA_KEY_DIM
    qscale = GLA_KEY_HEAD_DIM ** -0.5
    n_chunks = tb // gc
    gla_proj = [proj_task(pc_s, win_ref, OFF_GLA, lo) for lo in range(0, GLA_COLS, MXU_COLS)]
    ga_down = _dot(h_s[...], win_ref[:, W_GA:W_COLS])
    logits = _dot(ga_down.astype(BF16), wa2_ref[...]) + ba_ref[...]
    for task in gla_proj[:3]:
        task()
    log_a = _log_sigmoid(logits) * (LOG2E / GLA_GATE_TAU)
    ri = lax.broadcasted_iota(jnp.int32, (gc, gc), 0)
    ci = lax.broadcasted_iota(jnp.int32, (gc, gc), 1)
    tril = jnp.where(ri >= ci, 1.0, 0.0).astype(BF16)
    for c in range(n_chunks):
        g = log_a[c * gc:(c + 1) * gc, :]
        g_hi = g.astype(BF16)
        g_lo = (g - g_hi.astype(F32)).astype(BF16)
        cum_s[c * gc:(c + 1) * gc, :] = _dot(tril, g_hi) + _dot(tril, g_lo)
    for task in gla_proj[3:] + ret_proj[:2]:
        task()

    def diag_unit(u):
        r0 = u * GLA_SUB
        qb = pc_s[r0:r0 + GLA_SUB, 0:kd] * qscale
        cb = cum_s[r0:r0 + GLA_SUB, :]
        for j in range(GLA_SUB):
            kj = pc_s[r0 + j:r0 + j + 1, kd:2 * kd]
            cj = cum_s[r0 + j:r0 + j + 1, :]
            tj = qb * kj * jnp.exp2(jnp.minimum(cb - cj, 0.0))
            td_s[r0:r0 + GLA_SUB, j * kd:(j + 1) * kd] = tj.astype(BF16)

    interleave([functools.partial(diag_unit, u) for u in range(tb // GLA_SUB)], ret_proj[2:] + conv_proj[:4])

    def conv_task():
        z = pa_s[:, 2 * CONV_DIM:3 * CONV_DIM] * pa_s[:, 0:CONV_DIM]
        zc_s[SUBLANES:SUBLANES + tb, :] = z
        z1 = zc_s[SUBLANES - 1:SUBLANES - 1 + tb, :]
        z2 = zc_s[SUBLANES - 2:SUBLANES - 2 + tb, :]
        cw = convw_ref[...]
        y_a = pa_s[:, CONV_DIM:2 * CONV_DIM] * (cw[0:1, :] * z2 + cw[1:2, :] * z1 + cw[2:3, :] * z)
        zc_s[0:SUBLANES, :] = z[tb - SUBLANES:tb, :]
        m_s[...] = gate_s[:, 0:D_MODEL] * _dot(y_a.astype(BF16), wba_ref[...])

    ret = [dict() for _ in range(RET_HEADS)]

    def ret_rotary(hh):
        lo = hh * RET_HEAD_DIM
        cosf = cos_ref[0]
        sins = sin_ref[0]
        qh = pb_s[:, lo:lo + RET_HEAD_DIM]
        kh = pb_s[:, RET_DIM + lo:RET_DIM + lo + RET_HEAD_DIM]
        qc = (qh * cosf + pltpu.roll(qh, RET_HEAD_DIM // 2, axis=1) * sins) * (RET_HEAD_DIM ** -0.5)
        kc = kh * cosf + pltpu.roll(kh, RET_HEAD_DIM // 2, axis=1) * sins
        ret[hh].update(q=qc.astype(BF16), k=kc.astype(BF16),
                       qd=(qc * rdec_s[2 * hh]).astype(BF16), kd=(kc * rdec_s[2 * hh + 1]).astype(BF16),
                       v=pb_s[:, 2 * RET_DIM + lo:2 * RET_DIM + lo + RET_HEAD_DIM].astype(BF16))

    def ret_scores(hh):
        r = ret[hh]
        log_g = math.log(1.0 - 2.0 ** (-5.0 - hh))
        state = rstate_s[hh]
        r["s"] = (_dot_nt(r["q"], r["k"]) * dmask_s[hh]).astype(BF16)
        r["inter"] = _dot(r["qd"], state.astype(BF16))
        rstate_s[hh] = state * math.exp(log_g * tb) + _dot_tn(r["kd"], r["v"])

    def ret_output(hh):
        r = ret[hh]
        lo = hh * RET_HEAD_DIM
        o = _dot(r["s"], r["v"]) + r["inter"]
        oc = o - jnp.mean(o, axis=-1, keepdims=True)
        ob_s[:, lo:lo + RET_HEAD_DIM] = oc * lax.rsqrt(jnp.mean(oc * oc, axis=-1, keepdims=True) + EPS)

    rowi = lax.broadcasted_iota(jnp.int32, (gc, LANES), 0)
    lanei = lax.broadcasted_iota(jnp.int32, (gc, LANES), 1)
    colj = lanei & (GLA_KEY_HEAD_DIM - 1)
    sub_shift = GLA_SUB.bit_length() - 1
    diag_mask = ((colj >> sub_shift) == (rowi >> sub_shift)) & (colj <= rowi)
    head_lo = lanei < GLA_KEY_HEAD_DIM
    srow = lax.broadcasted_iota(jnp.int32, (2 * GLA_VAL_HEAD_DIM, LANES), 0)
    slane = lax.broadcasted_iota(jnp.int32, (2 * GLA_VAL_HEAD_DIM, LANES), 1)
    state_mask = (srow < GLA_VAL_HEAD_DIM) == (slane < GLA_KEY_HEAD_DIM)
    zero_v = jnp.zeros((gc, GLA_VAL_HEAD_DIM), BF16)
    rowk = lax.broadcasted_iota(jnp.int32, (gc, kd), 0)

    gla = [[dict() for _ in range(GLA_HEADS // 2)] for _ in range(n_chunks)]

    def gla_intra(c):
        rows = slice(c * gc, (c + 1) * gc)
        cum = cum_s[rows, :]
        last = cum[gc - 1:gc, :]
        q = pc_s[rows, 0:kd] * qscale
        k = pc_s[rows, kd:2 * kd]
        q_in = q * jnp.exp2(cum)
        k_st = k * jnp.exp2(last - cum)
        a_parts, b_parts = [], []
        for jb in range(N_SUB - 1):
            dj = cum[(jb + 1) * GLA_SUB - 1:(jb + 1) * GLA_SUB, :]
            a_parts.append(jnp.where(rowk >= (jb + 1) * GLA_SUB,
                                     q * jnp.exp2(jnp.minimum(cum - dj, 0.0)), 0.0))
            b_parts.append(jnp.where((rowk >= jb * GLA_SUB) & (rowk < (jb + 1) * GLA_SUB),
                                     k * jnp.exp2(jnp.minimum(dj - cum, 0.0)), 0.0))
        for pr in range(GLA_HEADS // 2):
            ls = slice(pr * LANES, (pr + 1) * LANES)
            a_cat = jnp.concatenate([a[:, ls].astype(BF16) for a in a_parts], axis=1)
            b_lo = jnp.concatenate([jnp.where(head_lo, b[:, ls], 0.0).astype(BF16) for b in b_parts], axis=1)
            b_hi = jnp.concatenate([jnp.where(head_lo, 0.0, b[:, ls]).astype(BF16) for b in b_parts], axis=1)
            b_two = jnp.concatenate([b_lo, b_hi], axis=0)
            s_off = _dot_nt(a_cat, b_two)
            s = jnp.where(diag_mask, srep_s[rows, ls], s_off).astype(BF16)
            v_lo = 2 * kd + 2 * pr * GLA_VAL_HEAD_DIM
            v0 = pc_s[rows, v_lo:v_lo + GLA_VAL_HEAD_DIM].astype(BF16)
            v1 = pc_s[rows, v_lo + GLA_VAL_HEAD_DIM:v_lo + 2 * GLA_VAL_HEAD_DIM].astype(BF16)
            v_bd = jnp.concatenate([jnp.concatenate([v0, zero_v], axis=1),
                                    jnp.concatenate([zero_v, v1], axis=1)], axis=0)
            v_pair = jnp.concatenate([v0, v1], axis=1)
            upd = _dot_tn(v_pair, k_st[:, ls].astype(BF16))
            gla[c][pr].update(intra=_dot(s, v_bd), q_in=q_in[:, ls].astype(BF16),
                              upd=jnp.where(state_mask, upd, 0.0), decay=jnp.exp2(last[:, ls]))

    def gla_carry():
        for pr in range(GLA_HEADS // 2):
            st = gstate_s[pr]
            for c in range(n_chunks):
                g = gla[c][pr]
                o = g["intra"] + _dot_nt(g["q_in"], st.astype(BF16))
                st = st * g["decay"] + g["upd"]
                for hl in range(2):
                    oh = o[:, hl * GLA_VAL_HEAD_DIM:(hl + 1) * GLA_VAL_HEAD_DIM]
                    lo = (2 * pr + hl) * GLA_VAL_HEAD_DIM
                    oc_s[c * gc:(c + 1) * gc, lo:lo + GLA_VAL_HEAD_DIM] = (
                        oh * lax.rsqrt(jnp.mean(oh * oh, axis=-1, keepdims=True) + EPS))
            gstate_s[pr] = st

    def srep_task():
        srep_s[...] = _dot(td_s[...], emat_ref[...])

    heads = range(RET_HEADS)
    interleave([next_h_task(0), next_h_task(half_tb)] + [functools.partial(ret_rotary, hh) for hh in heads],
               conv_proj[4:])
    interleave([srep_task] + [functools.partial(ret_scores, hh) for hh in heads], gate_proj[:4])
    interleave([functools.partial(gla_intra, c) for c in range(n_chunks)], gate_proj[4:8])
    interleave([functools.partial(ret_output, hh) for hh in heads], gate_proj[8:])
    h_s[...] = hn_s[...]
    interleave([gla_carry], [conv_task])

    y_b = ob_s[...] * retgn_ref[...] * _silu(pb_s[:, 3 * RET_DIM:4 * RET_DIM])
    m_s[...] += gate_s[:, D_MODEL:2 * D_MODEL] * _dot(y_b.astype(BF16), wbb_ref[...])
    y_c = oc_s[...] * glagn_ref[...] * _silu(pc_s[:, 2 * kd + GLA_VAL_DIM:2 * kd + 2 * GLA_VAL_DIM])
    merged = m_s[...] + gate_s[:, 2 * D_MODEL:3 * D_MODEL] * _dot(y_c.astype(BF16), wbc_ref[...])

    merged_b = merged.astype(BF16)
    sumsq = jnp.zeros((tb, 1), F32)
    for lo in range(0, D_MODEL, MXU_COLS):
        blk = _dot(merged_b, wout_ref[:, lo:lo + MXU_COLS])
        m_s[:, lo:lo + MXU_COLS] = blk
        sumsq = sumsq + jnp.sum(blk * blk, axis=-1, keepdims=True)
    scale = lax.rsqrt(sumsq * (1.0 / D_MODEL) + EPS)
    out_ref[0] = x_ref[0] + m_s[...] * scale * npost_ref[...]


def _layer_spec(arr, layer, grid_rank):
    index = (lambda i: (layer, 0, 0)) if grid_rank == 1 else (lambda i, j: (layer, 0, 0))
    return pl.BlockSpec((None,) + arr.shape[1:], index, pipeline_mode=pl.Buffered(1))


def _diag_reduce_matrix():
    kidx = np.arange(GLA_SUB * GLA_KEY_DIM)
    j = kidx // GLA_KEY_DIM
    hk = (kidx % GLA_KEY_DIM) // GLA_KEY_HEAD_DIM
    col = np.arange(GLA_KEY_DIM)
    hc = col // GLA_KEY_HEAD_DIM
    cj = (col % GLA_KEY_HEAD_DIM) % GLA_SUB
    return jnp.asarray((hk[:, None] == hc[None, :]) & (j[:, None] == cj[None, :]), dtype=BF16)


def _mixer(x, cos_t, sin_t, layer_params, emat, layer, tb):
    b, t, d = x.shape
    nt = t // tb
    tok = lambda width: pl.BlockSpec((1, tb, width), lambda i, j: (i, j, 0))
    next_tok = pl.BlockSpec((1, tb, d), lambda i, j: (jnp.minimum(i + (j + 1) // nt, b - 1), (j + 1) % nt, 0))
    const_specs = [_layer_spec(p, layer, 2) for p in layer_params]
    const_specs.append(pl.BlockSpec(emat.shape, lambda i, j: (0, 0), pipeline_mode=pl.Buffered(1)))
    scratch = [
        pltpu.VMEM((tb, D_MODEL), BF16),
        pltpu.VMEM((tb, D_MODEL), BF16),
        pltpu.VMEM((tb, 3 * CONV_DIM), F32),
        pltpu.VMEM((tb, 4 * RET_DIM), F32),
        pltpu.VMEM((tb, GLA_COLS), F32),
        pltpu.VMEM((tb, N_BRANCHES * D_MODEL), F32),
        pltpu.VMEM((tb, RET_DIM), F32),
        pltpu.VMEM((tb, GLA_VAL_DIM), F32),
        pltpu.VMEM((tb, D_MODEL), F32),
        pltpu.VMEM((tb + SUBLANES, CONV_DIM), F32),
        pltpu.VMEM((RET_HEADS, RET_HEAD_DIM, RET_HEAD_DIM), F32),
        pltpu.VMEM((GLA_HEADS // 2, 2 * GLA_VAL_HEAD_DIM, LANES), F32),
        pltpu.VMEM((tb, GLA_KEY_DIM), F32),
        pltpu.VMEM((tb, GLA_SUB * GLA_KEY_DIM), BF16),
        pltpu.VMEM((tb, GLA_KEY_DIM), F32),
        pltpu.VMEM((RET_HEADS, tb, tb), F32),
        pltpu.VMEM((2 * RET_HEADS, tb, RET_HEAD_DIM), F32),
    ]
    return pl.pallas_call(
        functools.partial(_mixer_kernel, tb=tb),
        grid=(b, t // tb),
        in_specs=[tok(d), next_tok, tok(RET_HEAD_DIM), tok(RET_HEAD_DIM)] + const_specs,
        out_specs=tok(d),
        out_shape=jax.ShapeDtypeStruct(x.shape, x.dtype),
        scratch_shapes=scratch,
        compiler_params=pltpu.CompilerParams(dimension_semantics=("arbitrary", "arbitrary"),
                                             vmem_limit_bytes=VMEM_LIMIT_BYTES),
        name="mixer",
    )(x, x, cos_t, sin_t, *layer_params, emat)


def _ffn_kernel(x_ref, npre_ref, wg_ref, wu_ref, wd_ref, npost_ref, out_ref, *, fc):
    x = x_ref[...]
    h = _rms(x, npre_ref[...]).astype(BF16)
    acc = jnp.zeros(x.shape, F32)
    for f in range(0, D_FF, fc):
        g = _dot(h, wg_ref[:, f:f + fc])
        u = _dot(h, wu_ref[:, f:f + fc])
        acc = acc + _dot((_silu(g) * u).astype(BF16), wd_ref[f:f + fc, :])
    out_ref[...] = x + _rms(acc, npost_ref[...])


def _ffn(x2, layer_params, layer, tm, fc):
    n, d = x2.shape
    return pl.pallas_call(
        functools.partial(_ffn_kernel, fc=fc),
        grid=(n // tm,),
        in_specs=[pl.BlockSpec((tm, d), lambda i: (i, 0))] + [_layer_spec(p, layer, 1) for p in layer_params],
        out_specs=pl.BlockSpec((tm, d), lambda i: (i, 0)),
        out_shape=jax.ShapeDtypeStruct(x2.shape, x2.dtype),
        compiler_params=pltpu.CompilerParams(dimension_semantics=("arbitrary",),
                                             vmem_limit_bytes=VMEM_LIMIT_BYTES),
        name="swiglu",
    )(x2, *layer_params)


def _pick_block(t, want):
    tb = min(want, t)
    while t % tb:
        tb //= 2
    return tb


def kernel(x, positions, norm_mix_pre, w_in, conv_w, ret_gn_w, gla_w_a2, gla_b_a, gla_gn_w, w_branch_a, w_branch_b, w_branch_c, w_out, norm_mix_post, norm_ffn_pre, w_ffn_gate, w_ffn_up, w_ffn_down, norm_ffn_post):
    b, t, d = x.shape
    depth = w_in.shape[0]
    tb = _pick_block(t, 256)
    tm = _pick_block(b * t, 512)
    cos_t, sin_t = _rope_tables(positions, _pick_block(t, 512))
    emat = _diag_reduce_matrix()
    rows = lambda v: v.reshape(depth, 1, -1)
    pad_rank = LANES - GLA_GATE_RANK
    w_packed = jnp.concatenate(
        [w_in[:, :, :OFF_GA], w_in[:, :, OFF_GATES:],
         jnp.pad(w_in[:, :, OFF_GA:OFF_GATES], ((0, 0), (0, 0), (0, pad_rank)))], axis=2).astype(BF16)
    wa2 = jnp.pad(gla_w_a2, ((0, 0), (0, pad_rank), (0, 0))).astype(BF16)
    mixer_params = [rows(norm_mix_pre), w_packed, conv_w, rows(ret_gn_w), wa2, rows(gla_b_a), rows(gla_gn_w),
                    w_branch_a.astype(BF16), w_branch_b.astype(BF16), w_branch_c.astype(BF16),
                    w_out.astype(BF16), rows(norm_mix_post)]
    ffn_params = [rows(norm_ffn_pre), w_ffn_gate.astype(BF16), w_ffn_up.astype(BF16),
                  w_ffn_down.astype(BF16), rows(norm_ffn_post)]
    for l in range(depth):
        x = _mixer(x, cos_t, sin_t, mixer_params, emat, l, tb)
        x = _ffn(x.reshape(b * t, d), ffn_params, l, tm, 256).reshape(b, t, d)
    return x
```

```python
import functools
import math

import jax
import jax.numpy as jnp
import numpy as np
from jax import lax
from jax.experimental import pallas as pl
from jax.experimental.pallas import tpu as pltpu

D_MODEL = 1024
CONV_DIM = 512
CONV_WIDTH = 3
RET_HEADS = 4
RET_HEAD_DIM = 128
RET_DIM = RET_HEADS * RET_HEAD_DIM
GLA_HEADS = 4
GLA_KEY_HEAD_DIM = 64
GLA_VAL_HEAD_DIM = 128
GLA_KEY_DIM = GLA_HEADS * GLA_KEY_HEAD_DIM
GLA_VAL_DIM = GLA_HEADS * GLA_VAL_HEAD_DIM
GLA_GATE_RANK = 16
GLA_GATE_TAU = 16.0
N_BRANCHES = 3
D_FF = 2816
ROPE_BASE = 10000.0
EPS = 1e-6
LOG2E = 1.0 / math.log(2.0)

OFF_CONV = 0
OFF_RET = 3 * CONV_DIM
OFF_GLA = OFF_RET + 4 * RET_DIM
OFF_GA = OFF_GLA + 2 * GLA_KEY_DIM + 2 * GLA_VAL_DIM
OFF_GATES = OFF_GA + GLA_GATE_RANK
GLA_COLS = 2 * GLA_KEY_DIM + 2 * GLA_VAL_DIM

LANES = 128
W_GATES = OFF_GA
W_GA = W_GATES + N_BRANCHES * D_MODEL
W_COLS = W_GA + LANES
SUBLANES = 8
MXU_COLS = 256
GLA_CHUNK = 64
GLA_SUB = 16
N_SUB = GLA_CHUNK // GLA_SUB
VMEM_LIMIT_BYTES = 56 * 1024 * 1024

BF16 = jnp.bfloat16
F32 = jnp.float32


def _dot(a, b):
    return jnp.dot(a, b, preferred_element_type=F32)


def _dot_nt(a, b):
    return lax.dot_general(a, b, (((1,), (1,)), ((), ())), preferred_element_type=F32)


def _dot_tn(a, b):
    return lax.dot_general(a, b, (((0,), (0,)), ((), ())), preferred_element_type=F32)


def _rms(x, w):
    return x * lax.rsqrt(jnp.mean(x * x, axis=-1, keepdims=True) + EPS) * w


def _sigmoid(x):
    return 1.0 / (1.0 + jnp.exp2(x * -LOG2E))


def _silu(x):
    return x * _sigmoid(x)


def _log_sigmoid(x):
    return jnp.minimum(x, 0.0) - jnp.log(1.0 + jnp.exp2(jnp.abs(x) * -LOG2E))


def _rope_kernel(pos_ref, invf_ref, cos_ref, sin_ref):
    tb = pos_ref.shape[1]
    hb = tb // 2
    half = RET_HEAD_DIM // 2
    pos = pos_ref[0].astype(F32)
    lo = lax.broadcasted_iota(jnp.int32, (hb, RET_HEAD_DIM), 1) < half
    ang = jnp.where(lo, pos[0:hb], pos[hb:tb]) * invf_ref[...]
    c = jnp.cos(ang)
    s = jnp.sin(ang)
    c_sw = pltpu.roll(c, half, axis=1)
    s_sw = pltpu.roll(s, half, axis=1)
    sign = jnp.where(lo, -1.0, 1.0)
    cos_ref[0, 0:hb, :] = jnp.where(lo, c, c_sw)
    cos_ref[0, hb:tb, :] = jnp.where(lo, c_sw, c)
    sin_ref[0, 0:hb, :] = jnp.where(lo, s, s_sw) * sign
    sin_ref[0, hb:tb, :] = jnp.where(lo, s_sw, s) * sign


def _rope_tables(positions, tb):
    b, t = positions.shape
    half = RET_HEAD_DIM // 2
    inv_freq = ROPE_BASE ** (-jnp.arange(half, dtype=F32) / half)
    invf = jnp.concatenate([inv_freq, inv_freq]).reshape(1, RET_HEAD_DIM)
    pos3 = positions.reshape(b, t, 1)
    out = jax.ShapeDtypeStruct((b, t, RET_HEAD_DIM), F32)
    return pl.pallas_call(
        _rope_kernel,
        grid=(b, t // tb),
        in_specs=[pl.BlockSpec((1, tb, 1), lambda i, j: (i, j, 0)),
                  pl.BlockSpec((1, RET_HEAD_DIM), lambda i, j: (0, 0))],
        out_specs=[pl.BlockSpec((1, tb, RET_HEAD_DIM), lambda i, j: (i, j, 0)),
                   pl.BlockSpec((1, tb, RET_HEAD_DIM), lambda i, j: (i, j, 0))],
        out_shape=[out, out],
        name="rope_tables",
    )(pos3, invf)


def _repack_kernel(w_ref, out_ref):
    rb = w_ref.shape[0]
    for lo in range(0, OFF_GA, LANES):
        out_ref[:, lo:lo + LANES] = w_ref[:, lo:lo + LANES].astype(BF16)
    for lo in range(0, N_BRANCHES * D_MODEL, LANES):
        out_ref[:, W_GATES + lo:W_GATES + lo + LANES] = w_ref[:, OFF_GATES + lo:OFF_GATES + lo + LANES].astype(BF16)
    ga = w_ref[:, OFF_GA:OFF_GATES]
    out_ref[:, W_GA:W_COLS] = jnp.concatenate(
        [ga, jnp.zeros((rb, LANES - GLA_GATE_RANK), F32)], axis=1).astype(BF16)


def _repack_w_in(w_in, rb=256):
    depth, d, cols = w_in.shape
    return pl.pallas_call(
        _repack_kernel,
        grid=(depth, d // rb),
        in_specs=[pl.BlockSpec((None, rb, cols), lambda l, i: (l, i, 0))],
        out_specs=pl.BlockSpec((None, rb, W_COLS), lambda l, i: (l, i, 0)),
        out_shape=jax.ShapeDtypeStruct((depth, d, W_COLS), BF16),
        compiler_params=pltpu.CompilerParams(dimension_semantics=("arbitrary", "arbitrary"),
                                             vmem_limit_bytes=VMEM_LIMIT_BYTES),
        name="repack_w_in",
    )(w_in)


def _mixer_kernel(x_ref, xnext_ref, cos_ref, sin_ref, npre_ref, win_ref, convw_ref,
                  retgn_ref, wa2_ref, ba_ref, glagn_ref, wba_ref, wbb_ref, wbc_ref, wout_ref,
                  npost_ref, emat_ref, out_ref,
                  h_s, hn_s, pa_s, pb_s, pc_s, gate_s, ob_s, oc_s, m_s, zc_s, rstate_s, gstate_s, cum_s,
                  td_s, srep_s, dmask_s, rdec_s, *, tb):
    b_idx = pl.program_id(0)
    t_idx = pl.program_id(1)

    @pl.when((b_idx == 0) & (t_idx == 0))
    def _():
        ri = lax.broadcasted_iota(jnp.int32, (tb, tb), 0)
        ci = lax.broadcasted_iota(jnp.int32, (tb, tb), 1)
        dif = jnp.maximum(ri - ci, 0).astype(F32)
        rowf = lax.broadcasted_iota(jnp.int32, (tb, RET_HEAD_DIM), 0).astype(F32)
        for hh in range(RET_HEADS):
            log_g = math.log(1.0 - 2.0 ** (-5.0 - hh))
            dmask_s[hh] = jnp.where(ri >= ci, jnp.exp(log_g * dif), 0.0)
            rdec_s[2 * hh] = jnp.exp(log_g * (rowf + 1.0))
            rdec_s[2 * hh + 1] = jnp.exp(log_g * (tb - 1.0 - rowf))

    @pl.when(t_idx == 0)
    def _():
        zc_s[0:SUBLANES, :] = jnp.zeros((SUBLANES, CONV_DIM), F32)
        rstate_s[...] = jnp.zeros(rstate_s.shape, F32)
        gstate_s[...] = jnp.zeros(gstate_s.shape, F32)

    @pl.when((b_idx == 0) & (t_idx == 0))
    def _():
        h_s[...] = _rms(x_ref[0], npre_ref[...]).astype(BF16)

    half_tb = tb // 2

    def next_h_task(r0):
        def run():
            xn = xnext_ref[0, r0:r0 + half_tb, :]
            hn_s[r0:r0 + half_tb, :] = _rms(xn, npre_ref[...]).astype(BF16)
        return run

    def proj_task(dst_ref, w_ref, w_lo, lo, act=None):
        def run():
            r = _dot(h_s[...], w_ref[:, w_lo + lo:w_lo + lo + MXU_COLS])
            dst_ref[:, lo:lo + MXU_COLS] = r if act is None else act(r)
        return run

    def interleave(primary, fillers):
        done = 0
        for i, step in enumerate(primary):
            step()
            while done * len(primary) < (i + 1) * len(fillers):
                fillers[done]()
                done += 1

    ret_proj = [proj_task(pb_s, win_ref, OFF_RET, lo) for lo in range(0, 4 * RET_DIM, MXU_COLS)]
    conv_proj = [proj_task(pa_s, win_ref, OFF_CONV, lo) for lo in range(0, 3 * CONV_DIM, MXU_COLS)]
    gate_proj = [proj_task(gate_s, win_ref, W_GATES, lo, _sigmoid) for lo in range(0, N_BRANCHES * D_MODEL, MXU_COLS)]

    gc = GLA_CHUNK
    kd = GLA_KEY_DIM
    qscale = GLA_KEY_HEAD_DIM ** -0.5
    n_chunks = tb // gc
    gla_proj = [proj_task(pc_s, win_ref, OFF_GLA, lo) for lo in range(0, GLA_COLS, MXU_COLS)]
    ga_down = _dot(h_s[...], win_ref[:, W_GA:W_COLS])
    logits = _dot(ga_down.astype(BF16), wa2_ref[...]) + ba_ref[...]
    for task in gla_proj[:3]:
        task()
    log_a = _log_sigmoid(logits) * (LOG2E / GLA_GATE_TAU)
    ri = lax.broadcasted_iota(jnp.int32, (gc, gc), 0)
    ci = lax.broadcasted_iota(jnp.int32, (gc, gc), 1)
    tril = jnp.where(ri >= ci, 1.0, 0.0).astype(BF16)
    for c in range(n_chunks):
        g = log_a[c * gc:(c + 1) * gc, :]
        g_hi = g.astype(BF16)
        g_lo = (g - g_hi.astype(F32)).astype(BF16)
        cum_s[c * gc:(c + 1) * gc, :] = _dot(tril, g_hi) + _dot(tril, g_lo)
    for task in gla_proj[3:] + ret_proj[:2]:
        task()

    def diag_unit(u):
        r0 = u * GLA_SUB
        qb = pc_s[r0:r0 + GLA_SUB, 0:kd] * qscale
        cb = cum_s[r0:r0 + GLA_SUB, :]
        for j in range(GLA_SUB):
            kj = pc_s[r0 + j:r0 + j + 1, kd:2 * kd]
            cj = cum_s[r0 + j:r0 + j + 1, :]
            tj = qb * kj * jnp.exp2(jnp.minimum(cb - cj, 0.0))
            td_s[r0:r0 + GLA_SUB, j * kd:(j + 1) * kd] = tj.astype(BF16)

    interleave([functools.partial(diag_unit, u) for u in range(tb // GLA_SUB)], ret_proj[2:] + conv_proj[:4])

    def conv_task():
        z = pa_s[:, 2 * CONV_DIM:3 * CONV_DIM] * pa_s[:, 0:CONV_DIM]
        zc_s[SUBLANES:SUBLANES + tb, :] = z
        z1 = zc_s[SUBLANES - 1:SUBLANES - 1 + tb, :]
        z2 = zc_s[SUBLANES - 2:SUBLANES - 2 + tb, :]
        cw = convw_ref[...]
        y_a = pa_s[:, CONV_DIM:2 * CONV_DIM] * (cw[0:1, :] * z2 + cw[1:2, :] * z1 + cw[2:3, :] * z)
        zc_s[0:SUBLANES, :] = z[tb - SUBLANES:tb, :]
        m_s[...] = gate_s[:, 0:D_MODEL] * _dot(y_a.astype(BF16), wba_ref[...])

    ret = [dict() for _ in range(RET_HEADS)]

    def ret_rotary(hh):
        lo = hh * RET_HEAD_DIM
        cosf = cos_ref[0]
        sins = sin_ref[0]
        qh = pb_s[:, lo:lo + RET_HEAD_DIM]
        kh = pb_s[:, RET_DIM + lo:RET_DIM + lo + RET_HEAD_DIM]
        qc = (qh * cosf + pltpu.roll(qh, RET_HEAD_DIM // 2, axis=1) * sins) * (RET_HEAD_DIM ** -0.5)
        kc = kh * cosf + pltpu.roll(kh, RET_HEAD_DIM // 2, axis=1) * sins
        ret[hh].update(q=qc.astype(BF16), k=kc.astype(BF16),
                       qd=(qc * rdec_s[2 * hh]).astype(BF16), kd=(kc * rdec_s[2 * hh + 1]).astype(BF16),
                       v=pb_s[:, 2 * RET_DIM + lo:2 * RET_DIM + lo + RET_HEAD_DIM].astype(BF16))

    def ret_scores(hh):
        r = ret[hh]
        log_g = math.log(1.0 - 2.0 ** (-5.0 - hh))
        state = rstate_s[hh]
        r["s"] = (_dot_nt(r["q"], r["k"]) * dmask_s[hh]).astype(BF16)
        r["inter"] = _dot(r["qd"], state.astype(BF16))
        rstate_s[hh] = state * math.exp(log_g * tb) + _dot_tn(r["kd"], r["v"])

    def ret_output(hh):
        r = ret[hh]
        lo = hh * RET_HEAD_DIM
        o = _dot(r["s"], r["v"]) + r["inter"]
        oc = o - jnp.mean(o, axis=-1, keepdims=True)
        ob_s[:, lo:lo + RET_HEAD_DIM] = oc * lax.rsqrt(jnp.mean(oc * oc, axis=-1, keepdims=True) + EPS)

    rowi = lax.broadcasted_iota(jnp.int32, (gc, LANES), 0)
    lanei = lax.broadcasted_iota(jnp.int32, (gc, LANES), 1)
    colj = lanei & (GLA_KEY_HEAD_DIM - 1)
    sub_shift = GLA_SUB.bit_length() - 1
    diag_mask = ((colj >> sub_shift) == (rowi >> sub_shift)) & (colj <= rowi)
    head_lo = lanei < GLA_KEY_HEAD_DIM
    srow = lax.broadcasted_iota(jnp.int32, (2 * GLA_VAL_HEAD_DIM, LANES), 0)
    slane = lax.broadcasted_iota(jnp.int32, (2 * GLA_VAL_HEAD_DIM, LANES), 1)
    state_mask = (srow < GLA_VAL_HEAD_DIM) == (slane < GLA_KEY_HEAD_DIM)
    zero_v = jnp.zeros((gc, GLA_VAL_HEAD_DIM), BF16)
    rowk = lax.broadcasted_iota(jnp.int32, (gc, kd), 0)

    gla = [[dict() for _ in range(GLA_HEADS // 2)] for _ in range(n_chunks)]

    def gla_intra(c):
        rows = slice(c * gc, (c + 1) * gc)
        cum = cum_s[rows, :]
        last = cum[gc - 1:gc, :]
        q = pc_s[rows, 0:kd] * qscale
        k = pc_s[rows, kd:2 * kd]
        q_in = q * jnp.exp2(cum)
        k_st = k * jnp.exp2(last - cum)
        a_parts, b_parts = [], []
        for jb in range(N_SUB - 1):
            dj = cum[(jb + 1) * GLA_SUB - 1:(jb + 1) * GLA_SUB, :]
            a_parts.append(jnp.where(rowk >= (jb + 1) * GLA_SUB,
                                     q * jnp.exp2(jnp.minimum(cum - dj, 0.0)), 0.0))
            b_parts.append(jnp.where((rowk >= jb * GLA_SUB) & (rowk < (jb + 1) * GLA_SUB),
                                     k * jnp.exp2(jnp.minimum(dj - cum, 0.0)), 0.0))
        for pr in range(GLA_HEADS // 2):
            ls = slice(pr * LANES, (pr + 1) * LANES)
            a_cat = jnp.concatenate([a[:, ls].astype(BF16) for a in a_parts], axis=1)
            b_lo = jnp.concatenate([jnp.where(head_lo, b[:, ls], 0.0).astype(BF16) for b in b_parts], axis=1)
            b_hi = jnp.concatenate([jnp.where(head_lo, 0.0, b[:, ls]).astype(BF16) for b in b_parts], axis=1)
            b_two = jnp.concatenate([b_lo, b_hi], axis=0)
            s_off = _dot_nt(a_cat, b_two)
            s = jnp.where(diag_mask, srep_s[rows, ls], s_off).astype(BF16)
            v_lo = 2 * kd + 2 * pr * GLA_VAL_HEAD_DIM
            v0 = pc_s[rows, v_lo:v_lo + GLA_VAL_HEAD_DIM].astype(BF16)
            v1 = pc_s[rows, v_lo + GLA_VAL_HEAD_DIM:v_lo + 2 * GLA_VAL_HEAD_DIM].astype(BF16)
            v_bd = jnp.concatenate([jnp.concatenate([v0, zero_v], axis=1),
                                    jnp.concatenate([zero_v, v1], axis=1)], axis=0)
            v_pair = jnp.concatenate([v0, v1], axis=1)
            upd = _dot_tn(v_pair, k_st[:, ls].astype(BF16))
            gla[c][pr].update(intra=_dot(s, v_bd), q_in=q_in[:, ls].astype(BF16),
                              upd=jnp.where(state_mask, upd, 0.0), decay=jnp.exp2(last[:, ls]))

    def gla_carry():
        for pr in range(GLA_HEADS // 2):
            st = gstate_s[pr]
            for c in range(n_chunks):
                g = gla[c][pr]
                o = g["intra"] + _dot_nt(g["q_in"], st.astype(BF16))
                st = st * g["decay"] + g["upd"]
                for hl in range(2):
                    oh = o[:, hl * GLA_VAL_HEAD_DIM:(hl + 1) * GLA_VAL_HEAD_DIM]
                    lo = (2 * pr + hl) * GLA_VAL_HEAD_DIM
                    oc_s[c * gc:(c + 1) * gc, lo:lo + GLA_VAL_HEAD_DIM] = (
                        oh * lax.rsqrt(jnp.mean(oh * oh, axis=-1, keepdims=True) + EPS))
            gstate_s[pr] = st

    def srep_task():
        srep_s[...] = _dot(td_s[...], emat_ref[...])

    heads = range(RET_HEADS)
    interleave([next_h_task(0), next_h_task(half_tb)] + [functools.partial(ret_rotary, hh) for hh in heads],
               conv_proj[4:])
    interleave([srep_task] + [functools.partial(ret_scores, hh) for hh in heads], gate_proj[:4])
    interleave([functools.partial(gla_intra, c) for c in range(n_chunks)], gate_proj[4:8])
    interleave([functools.partial(ret_output, hh) for hh in heads], gate_proj[8:])
    h_s[...] = hn_s[...]
    interleave([gla_carry], [conv_task])

    y_b = ob_s[...] * retgn_ref[...] * _silu(pb_s[:, 3 * RET_DIM:4 * RET_DIM])
    m_s[...] += gate_s[:, D_MODEL:2 * D_MODEL] * _dot(y_b.astype(BF16), wbb_ref[...])
    y_c = oc_s[...] * glagn_ref[...] * _silu(pc_s[:, 2 * kd + GLA_VAL_DIM:2 * kd + 2 * GLA_VAL_DIM])
    merged = m_s[...] + gate_s[:, 2 * D_MODEL:3 * D_MODEL] * _dot(y_c.astype(BF16), wbc_ref[...])

    merged_b = merged.astype(BF16)
    sumsq = jnp.zeros((tb, 1), F32)
    for lo in range(0, D_MODEL, MXU_COLS):
        blk = _dot(merged_b, wout_ref[:, lo:lo + MXU_COLS])
        m_s[:, lo:lo + MXU_COLS] = blk
        sumsq = sumsq + jnp.sum(blk * blk, axis=-1, keepdims=True)
    scale = lax.rsqrt(sumsq * (1.0 / D_MODEL) + EPS)
    out_ref[0] = x_ref[0] + m_s[...] * scale * npost_ref[...]


def _layer_spec(arr, layer, grid_rank):
    index = (lambda i: (layer, 0, 0)) if grid_rank == 1 else (lambda i, j: (layer, 0, 0))
    return pl.BlockSpec((None,) + arr.shape[1:], index, pipeline_mode=pl.Buffered(1))


def _diag_reduce_matrix():
    kidx = np.arange(GLA_SUB * GLA_KEY_DIM)
    j = kidx // GLA_KEY_DIM
    hk = (kidx % GLA_KEY_DIM) // GLA_KEY_HEAD_DIM
    col = np.arange(GLA_KEY_DIM)
    hc = col // GLA_KEY_HEAD_DIM
    cj = (col % GLA_KEY_HEAD_DIM) % GLA_SUB
    return jnp.asarray((hk[:, None] == hc[None, :]) & (j[:, None] == cj[None, :]), dtype=BF16)


def _mixer(x, cos_t, sin_t, layer_params, emat, layer, tb):
    b, t, d = x.shape
    nt = t // tb
    tok = lambda width: pl.BlockSpec((1, tb, width), lambda i, j: (i, j, 0))
    next_tok = pl.BlockSpec((1, tb, d), lambda i, j: (jnp.minimum(i + (j + 1) // nt, b - 1), (j + 1) % nt, 0))
    const_specs = [_layer_spec(p, layer, 2) for p in layer_params]
    const_specs.append(pl.BlockSpec(emat.shape, lambda i, j: (0, 0), pipeline_mode=pl.Buffered(1)))
    scratch = [
        pltpu.VMEM((tb, D_MODEL), BF16),
        pltpu.VMEM((tb, D_MODEL), BF16),
        pltpu.VMEM((tb, 3 * CONV_DIM), F32),
        pltpu.VMEM((tb, 4 * RET_DIM), F32),
        pltpu.VMEM((tb, GLA_COLS), F32),
        pltpu.VMEM((tb, N_BRANCHES * D_MODEL), F32),
        pltpu.VMEM((tb, RET_DIM), F32),
        pltpu.VMEM((tb, GLA_VAL_DIM), F32),
        pltpu.VMEM((tb, D_MODEL), F32),
        pltpu.VMEM((tb + SUBLANES, CONV_DIM), F32),
        pltpu.VMEM((RET_HEADS, RET_HEAD_DIM, RET_HEAD_DIM), F32),
        pltpu.VMEM((GLA_HEADS // 2, 2 * GLA_VAL_HEAD_DIM, LANES), F32),
        pltpu.VMEM((tb, GLA_KEY_DIM), F32),
        pltpu.VMEM((tb, GLA_SUB * GLA_KEY_DIM), BF16),
        pltpu.VMEM((tb, GLA_KEY_DIM), F32),
        pltpu.VMEM((RET_HEADS, tb, tb), F32),
        pltpu.VMEM((2 * RET_HEADS, tb, RET_HEAD_DIM), F32),
    ]
    return pl.pallas_call(
        functools.partial(_mixer_kernel, tb=tb),
        grid=(b, t // tb),
        in_specs=[tok(d), next_tok, tok(RET_HEAD_DIM), tok(RET_HEAD_DIM)] + const_specs,
        out_specs=tok(d),
        out_shape=jax.ShapeDtypeStruct(x.shape, x.dtype),
        scratch_shapes=scratch,
        compiler_params=pltpu.CompilerParams(dimension_semantics=("arbitrary", "arbitrary"),
                                             vmem_limit_bytes=VMEM_LIMIT_BYTES),
        name="mixer",
    )(x, x, cos_t, sin_t, *layer_params, emat)


def _ffn_kernel(x_ref, npre_ref, wg_ref, wu_ref, wd_ref, npost_ref, out_ref, *, fc):
    x = x_ref[...]
    h = _rms(x, npre_ref[...]).astype(BF16)
    acc = jnp.zeros(x.shape, F32)
    for f in range(0, D_FF, fc):
        g = _dot(h, wg_ref[:, f:f + fc])
        u = _dot(h, wu_ref[:, f:f + fc])
        acc = acc + _dot((_silu(g) * u).astype(BF16), wd_ref[f:f + fc, :])
    out_ref[...] = x + _rms(acc, npost_ref[...])


def _ffn(x2, layer_params, layer, tm, fc):
    n, d = x2.shape
    return pl.pallas_call(
        functools.partial(_ffn_kernel, fc=fc),
        grid=(n // tm,),
        in_specs=[pl.BlockSpec((tm, d), lambda i: (i, 0))] + [_layer_spec(p, layer, 1) for p in layer_params],
        out_specs=pl.BlockSpec((tm, d), lambda i: (i, 0)),
        out_shape=jax.ShapeDtypeStruct(x2.shape, x2.dtype),
        compiler_params=pltpu.CompilerParams(dimension_semantics=("arbitrary",),
                                             vmem_limit_bytes=VMEM_LIMIT_BYTES),
        name="swiglu",
    )(x2, *layer_params)


def _pick_block(t, want):
    tb = min(want, t)
    while t % tb:
        tb //= 2
    return tb


def kernel(x, positions, norm_mix_pre, w_in, conv_w, ret_gn_w, gla_w_a2, gla_b_a, gla_gn_w, w_branch_a, w_branch_b, w_branch_c, w_out, norm_mix_post, norm_ffn_pre, w_ffn_gate, w_ffn_up, w_ffn_down, norm_ffn_post):
    b, t, d = x.shape
    depth = w_in.shape[0]
    tb = _pick_block(t, 256)
    tm = _pick_block(b * t, 512)
    cos_t, sin_t = _rope_tables(positions, _pick_block(t, 2048))
    emat = _diag_reduce_matrix()
    rows = lambda v: v.reshape(depth, 1, -1)
    pad_rank = LANES - GLA_GATE_RANK
    w_packed = _repack_w_in(w_in)
    wa2 = jnp.pad(gla_w_a2, ((0, 0), (0, pad_rank), (0, 0))).astype(BF16)
    mixer_params = [rows(norm_mix_pre), w_packed, conv_w, rows(ret_gn_w), wa2, rows(gla_b_a), rows(gla_gn_w),
                    w_branch_a.astype(BF16), w_branch_b.astype(BF16), w_branch_c.astype(BF16),
                    w_out.astype(BF16), rows(norm_mix_post)]
    ffn_params = [rows(norm_ffn_pre), w_ffn_gate.astype(BF16), w_ffn_up.astype(BF16),
                  w_ffn_down.astype(BF16), rows(norm_ffn_post)]
    for l in range(depth):
        x = _mixer(x, cos_t, sin_t, mixer_params, emat, l, tb)
        x = _ffn(x.reshape(b * t, d), ffn_params, l, tm, 256).reshape(b, t, d)
    return x
```

```python
import functools
import math

import jax
import jax.numpy as jnp
import numpy as np
from jax import lax
from jax.experimental import pallas as pl
from jax.experimental.pallas import tpu as pltpu

D_MODEL = 1024
CONV_DIM = 512
CONV_WIDTH = 3
RET_HEADS = 4
RET_HEAD_DIM = 128
RET_DIM = RET_HEADS * RET_HEAD_DIM
GLA_HEADS = 4
GLA_KEY_HEAD_DIM = 64
GLA_VAL_HEAD_DIM = 128
GLA_KEY_DIM = GLA_HEADS * GLA_KEY_HEAD_DIM
GLA_VAL_DIM = GLA_HEADS * GLA_VAL_HEAD_DIM
GLA_GATE_RANK = 16
GLA_GATE_TAU = 16.0
N_BRANCHES = 3
D_FF = 2816
ROPE_BASE = 10000.0
EPS = 1e-6
LOG2E = 1.0 / math.log(2.0)

OFF_CONV = 0
OFF_RET = 3 * CONV_DIM
OFF_GLA = OFF_RET + 4 * RET_DIM
OFF_GA = OFF_GLA + 2 * GLA_KEY_DIM + 2 * GLA_VAL_DIM
OFF_GATES = OFF_GA + GLA_GATE_RANK
GLA_COLS = 2 * GLA_KEY_DIM + 2 * GLA_VAL_DIM

LANES = 128
W_GATES = OFF_GA
W_GA = W_GATES + N_BRANCHES * D_MODEL
W_COLS = W_GA + LANES
SUBLANES = 8
MXU_COLS = 256
GLA_CHUNK = 64
GLA_SUB = 8
DIAG_TILE = 16
VMEM_LIMIT_BYTES = 56 * 1024 * 1024
MIXER_BLOCK = 256
FFN_BLOCK = 512
ROPE_BLOCK = 2048

BF16 = jnp.bfloat16
F32 = jnp.float32


def _dot(a, b):
    return jnp.dot(a, b, preferred_element_type=F32)


def _dot_nt(a, b):
    return lax.dot_general(a, b, (((1,), (1,)), ((), ())), preferred_element_type=F32)


def _dot_tn(a, b):
    return lax.dot_general(a, b, (((0,), (0,)), ((), ())), preferred_element_type=F32)


def _rms(x, w):
    return x * lax.rsqrt(jnp.mean(x * x, axis=-1, keepdims=True) + EPS) * w


def _sigmoid(x):
    return 1.0 / (1.0 + jnp.exp2(x * -LOG2E))


def _silu(x):
    return x * _sigmoid(x)


def _log_sigmoid(x):
    return jnp.minimum(x, 0.0) - jnp.log(1.0 + jnp.exp2(jnp.abs(x) * -LOG2E))


def _rope_kernel(pos_ref, invf_ref, cos_ref, sin_ref):
    tb = pos_ref.shape[1]
    hb = tb // 2
    half = RET_HEAD_DIM // 2
    pos = pos_ref[0].astype(F32)
    lo = lax.broadcasted_iota(jnp.int32, (hb, RET_HEAD_DIM), 1) < half
    ang = jnp.where(lo, pos[0:hb], pos[hb:tb]) * invf_ref[...]
    c = jnp.cos(ang)
    s = jnp.sin(ang)
    c_sw = pltpu.roll(c, half, axis=1)
    s_sw = pltpu.roll(s, half, axis=1)
    sign = jnp.where(lo, -1.0, 1.0)
    cos_ref[0, 0:hb, :] = jnp.where(lo, c, c_sw)
    cos_ref[0, hb:tb, :] = jnp.where(lo, c_sw, c)
    sin_ref[0, 0:hb, :] = jnp.where(lo, s, s_sw) * sign
    sin_ref[0, hb:tb, :] = jnp.where(lo, s_sw, s) * sign


def _rope_tables(positions, tb):
    b, t = positions.shape
    half = RET_HEAD_DIM // 2
    inv_freq = ROPE_BASE ** (-jnp.arange(half, dtype=F32) / half)
    invf = jnp.concatenate([inv_freq, inv_freq]).reshape(1, RET_HEAD_DIM)
    pos3 = positions.reshape(b, t, 1)
    out = jax.ShapeDtypeStruct((b, t, RET_HEAD_DIM), F32)
    return pl.pallas_call(
        _rope_kernel,
        grid=(b, t // tb),
        in_specs=[pl.BlockSpec((1, tb, 1), lambda i, j: (i, j, 0)),
                  pl.BlockSpec((1, RET_HEAD_DIM), lambda i, j: (0, 0))],
        out_specs=[pl.BlockSpec((1, tb, RET_HEAD_DIM), lambda i, j: (i, j, 0)),
                   pl.BlockSpec((1, tb, RET_HEAD_DIM), lambda i, j: (i, j, 0))],
        out_shape=[out, out],
        name="rope_tables",
    )(pos3, invf)


def _mixer_kernel(x_ref, xnext_ref, cos_ref, sin_ref, npre_ref, win_ref, convw_ref,
                  retgn_ref, wa2_ref, ba_ref, glagn_ref, wba_ref, wbb_ref, wbc_ref, wout_ref,
                  npost_ref, emat_ref, out_ref,
                  h_s, hn_s, pa_s, pb_s, pc_s, gate_s, ob_s, oc_s, m_s, zc_s, rstate_s, gstate_s, cum_s,
                  td_s, srep_s, dmask_s, rdec_s, *, tb):
    b_idx = pl.program_id(0)
    t_idx = pl.program_id(1)

    @pl.when((b_idx == 0) & (t_idx == 0))
    def _():
        ri = lax.broadcasted_iota(jnp.int32, (tb, tb), 0)
        ci = lax.broadcasted_iota(jnp.int32, (tb, tb), 1)
        dif = jnp.maximum(ri - ci, 0).astype(F32)
        rowf = lax.broadcasted_iota(jnp.int32, (tb, RET_HEAD_DIM), 0).astype(F32)
        for hh in range(RET_HEADS):
            log_g = math.log(1.0 - 2.0 ** (-5.0 - hh))
            dmask_s[hh] = jnp.where(ri >= ci, jnp.exp(log_g * dif), 0.0)
            rdec_s[2 * hh] = jnp.exp(log_g * (rowf + 1.0))
            rdec_s[2 * hh + 1] = jnp.exp(log_g * (tb - 1.0 - rowf))

    @pl.when(t_idx == 0)
    def _():
        zc_s[0:SUBLANES, :] = jnp.zeros((SUBLANES, CONV_DIM), F32)
        rstate_s[...] = jnp.zeros(rstate_s.shape, F32)
        gstate_s[...] = jnp.zeros(gstate_s.shape, F32)

    @pl.when((b_idx == 0) & (t_idx == 0))
    def _():
        h_s[...] = _rms(x_ref[0], npre_ref[...]).astype(BF16)

    half_tb = tb // 2

    def next_h_task(r0):
        def run():
            xn = xnext_ref[0, r0:r0 + half_tb, :]
            hn_s[r0:r0 + half_tb, :] = _rms(xn, npre_ref[...]).astype(BF16)
        return run

    def proj_task(dst_ref, w_ref, w_lo, lo, act=None):
        def run():
            r = _dot(h_s[...], w_ref[:, w_lo + lo:w_lo + lo + MXU_COLS])
            dst_ref[:, lo:lo + MXU_COLS] = r if act is None else act(r)
        return run

    def interleave(primary, fillers):
        done = 0
        for i, step in enumerate(primary):
            step()
            while done * len(primary) < (i + 1) * len(fillers):
                fillers[done]()
                done += 1

    ret_proj = [proj_task(pb_s, win_ref, OFF_RET, lo) for lo in range(0, 4 * RET_DIM, MXU_COLS)]
    conv_proj = [proj_task(pa_s, win_ref, OFF_CONV, lo) for lo in range(0, 3 * CONV_DIM, MXU_COLS)]
    gate_proj = [proj_task(gate_s, win_ref, W_GATES, lo, _sigmoid) for lo in range(0, N_BRANCHES * D_MODEL, MXU_COLS)]

    gc = GLA_CHUNK
    kd = GLA_KEY_DIM
    qscale = GLA_KEY_HEAD_DIM ** -0.5
    n_chunks = tb // gc
    gla_proj = [proj_task(pc_s, win_ref, OFF_GLA, lo) for lo in range(0, GLA_COLS, MXU_COLS)]
    ga_down = _dot(h_s[...], win_ref[:, W_GA:W_COLS])
    logits = _dot(ga_down.astype(BF16), wa2_ref[...]) + ba_ref[...]
    for task in gla_proj[:3]:
        task()
    log_a = _log_sigmoid(logits) * (LOG2E / GLA_GATE_TAU)
    ri = lax.broadcasted_iota(jnp.int32, (gc, gc), 0)
    ci = lax.broadcasted_iota(jnp.int32, (gc, gc), 1)
    tril = jnp.where(ri >= ci, 1.0, 0.0).astype(BF16)
    for c in range(n_chunks):
        g = log_a[c * gc:(c + 1) * gc, :]
        g_hi = g.astype(BF16)
        g_lo = (g - g_hi.astype(F32)).astype(BF16)
        cum_s[c * gc:(c + 1) * gc, :] = _dot(tril, g_hi) + _dot(tril, g_lo)
    for task in gla_proj[3:] + ret_proj[:2]:
        task()

    def diag_unit(u):
        r0 = u * DIAG_TILE
        subs = []
        for rs in range(r0, r0 + DIAG_TILE, GLA_SUB):
            subs.append((rs, pc_s[rs:rs + GLA_SUB, 0:kd] * qscale, cum_s[rs:rs + GLA_SUB, :]))
        for j in range(GLA_SUB):
            tjs = []
            for rs, qb, cb in subs:
                kj = pc_s[rs + j:rs + j + 1, kd:2 * kd]
                cj = cum_s[rs + j:rs + j + 1, :]
                tjs.append(qb * kj * jnp.exp2(jnp.minimum(cb - cj, 0.0)))
            td_s[r0:r0 + DIAG_TILE, j * kd:(j + 1) * kd] = jnp.concatenate(tjs, axis=0).astype(BF16)

    interleave([functools.partial(diag_unit, u) for u in range(tb // DIAG_TILE)], ret_proj[2:] + conv_proj[:4])

    def conv_task():
        z = pa_s[:, 2 * CONV_DIM:3 * CONV_DIM] * pa_s[:, 0:CONV_DIM]
        zc_s[SUBLANES:SUBLANES + tb, :] = z
        z1 = zc_s[SUBLANES - 1:SUBLANES - 1 + tb, :]
        z2 = zc_s[SUBLANES - 2:SUBLANES - 2 + tb, :]
        cw = convw_ref[...]
        y_a = pa_s[:, CONV_DIM:2 * CONV_DIM] * (cw[0:1, :] * z2 + cw[1:2, :] * z1 + cw[2:3, :] * z)
        zc_s[0:SUBLANES, :] = z[tb - SUBLANES:tb, :]
        m_s[...] = gate_s[:, 0:D_MODEL] * _dot(y_a.astype(BF16), wba_ref[...])

    ret = [dict() for _ in range(RET_HEADS)]

    def ret_rotary(hh):
        lo = hh * RET_HEAD_DIM
        cosf = cos_ref[0]
        sins = sin_ref[0]
        qh = pb_s[:, lo:lo + RET_HEAD_DIM]
        kh = pb_s[:, RET_DIM + lo:RET_DIM + lo + RET_HEAD_DIM]
        qc = (qh * cosf + pltpu.roll(qh, RET_HEAD_DIM // 2, axis=1) * sins) * (RET_HEAD_DIM ** -0.5)
        kc = kh * cosf + pltpu.roll(kh, RET_HEAD_DIM // 2, axis=1) * sins
        ret[hh].update(q=qc.astype(BF16), k=kc.astype(BF16),
                       qd=(qc * rdec_s[2 * hh]).astype(BF16), kd=(kc * rdec_s[2 * hh + 1]).astype(BF16),
                       v=pb_s[:, 2 * RET_DIM + lo:2 * RET_DIM + lo + RET_HEAD_DIM].astype(BF16))

    def ret_scores(hh):
        r = ret[hh]
        log_g = math.log(1.0 - 2.0 ** (-5.0 - hh))
        state = rstate_s[hh]
        r["s"] = (_dot_nt(r["q"], r["k"]) * dmask_s[hh]).astype(BF16)
        r["inter"] = _dot(r["qd"], state.astype(BF16))
        rstate_s[hh] = state * math.exp(log_g * tb) + _dot_tn(r["kd"], r["v"])

    def ret_output(hh):
        r = ret[hh]
        lo = hh * RET_HEAD_DIM
        o = _dot(r["s"], r["v"]) + r["inter"]
        oc = o - jnp.mean(o, axis=-1, keepdims=True)
        ob_s[:, lo:lo + RET_HEAD_DIM] = oc * lax.rsqrt(jnp.mean(oc * oc, axis=-1, keepdims=True) + EPS)

    rowi = lax.broadcasted_iota(jnp.int32, (gc, LANES), 0)
    lanei = lax.broadcasted_iota(jnp.int32, (gc, LANES), 1)
    colj = lanei & (GLA_KEY_HEAD_DIM - 1)
    sub_shift = GLA_SUB.bit_length() - 1
    diag_mask = ((colj >> sub_shift) == (rowi >> sub_shift)) & (colj <= rowi)
    head_lo = lanei < GLA_KEY_HEAD_DIM
    srow = lax.broadcasted_iota(jnp.int32, (2 * GLA_VAL_HEAD_DIM, LANES), 0)
    slane = lax.broadcasted_iota(jnp.int32, (2 * GLA_VAL_HEAD_DIM, LANES), 1)
    state_mask = (srow < GLA_VAL_HEAD_DIM) == (slane < GLA_KEY_HEAD_DIM)
    zero_v = jnp.zeros((gc, GLA_VAL_HEAD_DIM), BF16)
    block_masks = {}
    half = gc // 4
    while half >= GLA_SUB:
        shift = (2 * half).bit_length() - 1
        block_masks[half] = (colj >> shift) == (rowi >> shift)
        half //= 2

    gla = [[dict() for _ in range(GLA_HEADS // 2)] for _ in range(n_chunks)]

    def gla_intra(c):
        rows = slice(c * gc, (c + 1) * gc)
        cum = cum_s[rows, :]
        last = cum[gc - 1:gc, :]
        q = pc_s[rows, 0:kd] * qscale
        k = pc_s[rows, kd:2 * kd]
        q_in = q * jnp.exp2(cum)
        k_st = k * jnp.exp2(last - cum)
        levels = []
        half = gc // 2
        while half >= GLA_SUB:
            a_rows, b_rows = [], []
            zero_half = jnp.zeros((half, kd), F32)
            for base in range(0, gc, 2 * half):
                mid = base + half
                ref = cum[mid - 1:mid, :]
                b_rows += [k[base:mid, :] * jnp.exp2(ref - cum[base:mid, :]), zero_half]
                a_rows += [zero_half, q[mid:mid + half, :] * jnp.exp2(cum[mid:mid + half, :] - ref)]
            levels.append((half, jnp.concatenate(a_rows, axis=0), jnp.concatenate(b_rows, axis=0)))
            half //= 2
        for pr in range(GLA_HEADS // 2):
            ls = slice(pr * LANES, (pr + 1) * LANES)
            s_off = None
            for half, a, b in levels:
                b_two = jnp.concatenate([jnp.where(head_lo, b[:, ls], 0.0).astype(BF16),
                                         jnp.where(head_lo, 0.0, b[:, ls]).astype(BF16)], axis=0)
                r = _dot_nt(a[:, ls].astype(BF16), b_two)
                if 2 * half < gc:
                    r = jnp.where(block_masks[half], r, 0.0)
                s_off = r if s_off is None else s_off + r
            s = jnp.where(diag_mask, srep_s[rows, ls], s_off).astype(BF16)
            v_lo = 2 * kd + 2 * pr * GLA_VAL_HEAD_DIM
            v0 = pc_s[rows, v_lo:v_lo + GLA_VAL_HEAD_DIM].astype(BF16)
            v1 = pc_s[rows, v_lo + GLA_VAL_HEAD_DIM:v_lo + 2 * GLA_VAL_HEAD_DIM].astype(BF16)
            v_bd = jnp.concatenate([jnp.concatenate([v0, zero_v], axis=1),
                                    jnp.concatenate([zero_v, v1], axis=1)], axis=0)
            v_pair = jnp.concatenate([v0, v1], axis=1)
            upd = _dot_tn(v_pair, k_st[:, ls].astype(BF16))
            gla[c][pr].update(intra=_dot(s, v_bd), q_in=q_in[:, ls].astype(BF16),
                              upd=jnp.where(state_mask, upd, 0.0), decay=jnp.exp2(last[:, ls]))

    def gla_carry():
        for pr in range(GLA_HEADS // 2):
            st = gstate_s[pr]
            for c in range(n_chunks):
                g = gla[c][pr]
                o = g["intra"] + _dot_nt(g["q_in"], st.astype(BF16))
                st = st * g["decay"] + g["upd"]
                for hl in range(2):
                    oh = o[:, hl * GLA_VAL_HEAD_DIM:(hl + 1) * GLA_VAL_HEAD_DIM]
                    lo = (2 * pr + hl) * GLA_VAL_HEAD_DIM
                    oc_s[c * gc:(c + 1) * gc, lo:lo + GLA_VAL_HEAD_DIM] = (
                        oh * lax.rsqrt(jnp.mean(oh * oh, axis=-1, keepdims=True) + EPS))
            gstate_s[pr] = st

    def srep_task():
        srep_s[...] = _dot(td_s[...], emat_ref[...])

    heads = range(RET_HEADS)
    interleave([next_h_task(0), next_h_task(half_tb)] + [functools.partial(ret_rotary, hh) for hh in heads],
               conv_proj[4:])
    interleave([srep_task] + [functools.partial(ret_scores, hh) for hh in heads], gate_proj[:4])
    interleave([functools.partial(gla_intra, c) for c in range(n_chunks)], gate_proj[4:8])
    interleave([functools.partial(ret_output, hh) for hh in heads], gate_proj[8:])
    h_s[...] = hn_s[...]
    interleave([gla_carry], [conv_task])

    y_b = ob_s[...] * retgn_ref[...] * _silu(pb_s[:, 3 * RET_DIM:4 * RET_DIM])
    m_s[...] += gate_s[:, D_MODEL:2 * D_MODEL] * _dot(y_b.astype(BF16), wbb_ref[...])
    y_c = oc_s[...] * glagn_ref[...] * _silu(pc_s[:, 2 * kd + GLA_VAL_DIM:2 * kd + 2 * GLA_VAL_DIM])
    merged = m_s[...] + gate_s[:, 2 * D_MODEL:3 * D_MODEL] * _dot(y_c.astype(BF16), wbc_ref[...])

    merged_b = merged.astype(BF16)
    sumsq = jnp.zeros((tb, 1), F32)
    for lo in range(0, D_MODEL, MXU_COLS):
        blk = _dot(merged_b, wout_ref[:, lo:lo + MXU_COLS])
        m_s[:, lo:lo + MXU_COLS] = blk
        sumsq = sumsq + jnp.sum(blk * blk, axis=-1, keepdims=True)
    scale = lax.rsqrt(sumsq * (1.0 / D_MODEL) + EPS)
    out_ref[0] = x_ref[0] + m_s[...] * scale * npost_ref[...]


def _layer_spec(arr, layer, grid_rank):
    index = (lambda i: (layer, 0, 0)) if grid_rank == 1 else (lambda i, j: (layer, 0, 0))
    return pl.BlockSpec((None,) + arr.shape[1:], index, pipeline_mode=pl.Buffered(1))


def _diag_reduce_matrix():
    kidx = np.arange(GLA_SUB * GLA_KEY_DIM)
    j = kidx // GLA_KEY_DIM
    hk = (kidx % GLA_KEY_DIM) // GLA_KEY_HEAD_DIM
    col = np.arange(GLA_KEY_DIM)
    hc = col // GLA_KEY_HEAD_DIM
    cj = (col % GLA_KEY_HEAD_DIM) % GLA_SUB
    return jnp.asarray((hk[:, None] == hc[None, :]) & (j[:, None] == cj[None, :]), dtype=BF16)


def _mixer(x, cos_t, sin_t, layer_params, emat, layer, tb):
    b, t, d = x.shape
    nt = t // tb
    tok = lambda width: pl.BlockSpec((1, tb, width), lambda i, j: (i, j, 0))
    next_tok = pl.BlockSpec((1, tb, d), lambda i, j: (jnp.minimum(i + (j + 1) // nt, b - 1), (j + 1) % nt, 0))
    const_specs = [_layer_spec(p, layer, 2) for p in layer_params]
    const_specs.append(pl.BlockSpec(emat.shape, lambda i, j: (0, 0), pipeline_mode=pl.Buffered(1)))
    scratch = [
        pltpu.VMEM((tb, D_MODEL), BF16),
        pltpu.VMEM((tb, D_MODEL), BF16),
        pltpu.VMEM((tb, 3 * CONV_DIM), F32),
        pltpu.VMEM((tb, 4 * RET_DIM), F32),
        pltpu.VMEM((tb, GLA_COLS), F32),
        pltpu.VMEM((tb, N_BRANCHES * D_MODEL), F32),
        pltpu.VMEM((tb, RET_DIM), F32),
        pltpu.VMEM((tb, GLA_VAL_DIM), F32),
        pltpu.VMEM((tb, D_MODEL), F32),
        pltpu.VMEM((tb + SUBLANES, CONV_DIM), F32),
        pltpu.VMEM((RET_HEADS, RET_HEAD_DIM, RET_HEAD_DIM), F32),
        pltpu.VMEM((GLA_HEADS // 2, 2 * GLA_VAL_HEAD_DIM, LANES), F32),
        pltpu.VMEM((tb, GLA_KEY_DIM), F32),
        pltpu.VMEM((tb, GLA_SUB * GLA_KEY_DIM), BF16),
        pltpu.VMEM((tb, GLA_KEY_DIM), F32),
        pltpu.VMEM((RET_HEADS, tb, tb), F32),
        pltpu.VMEM((2 * RET_HEADS, tb, RET_HEAD_DIM), F32),
    ]
    return pl.pallas_call(
        functools.partial(_mixer_kernel, tb=tb),
        grid=(b, t // tb),
        in_specs=[tok(d), next_tok, tok(RET_HEAD_DIM), tok(RET_HEAD_DIM)] + const_specs,
        out_specs=tok(d),
        out_shape=jax.ShapeDtypeStruct(x.shape, x.dtype),
        scratch_shapes=scratch,
        compiler_params=pltpu.CompilerParams(dimension_semantics=("arbitrary", "arbitrary"),
                                             vmem_limit_bytes=VMEM_LIMIT_BYTES),
        name="mixer",
    )(x, x, cos_t, sin_t, *layer_params, emat)


def _ffn_kernel(x_ref, npre_ref, wg_ref, wu_ref, wd_ref, npost_ref, out_ref, *, fc):
    x = x_ref[...]
    h = _rms(x, npre_ref[...]).astype(BF16)
    acc = jnp.zeros(x.shape, F32)
    for f in range(0, D_FF, fc):
        g = _dot(h, wg_ref[:, f:f + fc])
        u = _dot(h, wu_ref[:, f:f + fc])
        acc = acc + _dot((_silu(g) * u).astype(BF16), wd_ref[f:f + fc, :])
    out_ref[...] = x + _rms(acc, npost_ref[...])


def _ffn(x2, layer_params, layer, tm, fc):
    n, d = x2.shape
    return pl.pallas_call(
        functools.partial(_ffn_kernel, fc=fc),
        grid=(n // tm,),
        in_specs=[pl.BlockSpec((tm, d), lambda i: (i, 0))] + [_layer_spec(p, layer, 1) for p in layer_params],
        out_specs=pl.BlockSpec((tm, d), lambda i: (i, 0)),
        out_shape=jax.ShapeDtypeStruct(x2.shape, x2.dtype),
        compiler_params=pltpu.CompilerParams(dimension_semantics=("arbitrary",),
                                             vmem_limit_bytes=VMEM_LIMIT_BYTES),
        name="swiglu",
    )(x2, *layer_params)


def _pick_block(t, want):
    tb = min(want, t)
    while t % tb:
        tb //= 2
    return tb


def kernel(x, positions, norm_mix_pre, w_in, conv_w, ret_gn_w, gla_w_a2, gla_b_a, gla_gn_w, w_branch_a, w_branch_b, w_branch_c, w_out, norm_mix_post, norm_ffn_pre, w_ffn_gate, w_ffn_up, w_ffn_down, norm_ffn_post):
    b, t, d = x.shape
    depth = w_in.shape[0]
    tb = _pick_block(t, MIXER_BLOCK)
    tm = _pick_block(b * t, FFN_BLOCK)
    cos_t, sin_t = _rope_tables(positions, _pick_block(t, ROPE_BLOCK))
    emat = _diag_reduce_matrix()
    rows = lambda v: v.reshape(depth, 1, -1)
    pad_rank = LANES - GLA_GATE_RANK
    w_packed = jnp.concatenate(
        [w_in[:, :, :OFF_GA], w_in[:, :, OFF_GATES:],
         jnp.pad(w_in[:, :, OFF_GA:OFF_GATES], ((0, 0), (0, 0), (0, pad_rank)))], axis=2).astype(BF16)
    wa2 = jnp.pad(gla_w_a2, ((0, 0), (0, pad_rank), (0, 0))).astype(BF16)
    mixer_params = [rows(norm_mix_pre), w_packed, conv_w, rows(ret_gn_w), wa2, rows(gla_b_a), rows(gla_gn_w),
                    w_branch_a.astype(BF16), w_branch_b.astype(BF16), w_branch_c.astype(BF16),
                    w_out.astype(BF16), rows(norm_mix_post)]
    ffn_params = [rows(norm_ffn_pre), w_ffn_gate.astype(BF16), w_ffn_up.astype(BF16),
                  w_ffn_down.astype(BF16), rows(norm_ffn_post)]
    for l in range(depth):
        x = _mixer(x, cos_t, sin_t, mixer_params, emat, l, tb)
        x = _ffn(x.reshape(b * t, d), ffn_params, l, tm, MXU_COLS).reshape(b, t, d)
    return x
```

```python
import functools
import math

import jax
import jax.numpy as jnp
import numpy as np
from jax import lax
from jax.experimental import pallas as pl
from jax.experimental.pallas import tpu as pltpu

D_MODEL = 1024
CONV_DIM = 512
CONV_WIDTH = 3
RET_HEADS = 4
RET_HEAD_DIM = 128
RET_DIM = RET_HEADS * RET_HEAD_DIM
GLA_HEADS = 4
GLA_KEY_HEAD_DIM = 64
GLA_VAL_HEAD_DIM = 128
GLA_KEY_DIM = GLA_HEADS * GLA_KEY_HEAD_DIM
GLA_VAL_DIM = GLA_HEADS * GLA_VAL_HEAD_DIM
GLA_GATE_RANK = 16
GLA_GATE_TAU = 16.0
N_BRANCHES = 3
D_FF = 2816
ROPE_BASE = 10000.0
EPS = 1e-6
LOG2E = 1.0 / math.log(2.0)

OFF_CONV = 0
OFF_RET = 3 * CONV_DIM
OFF_GLA = OFF_RET + 4 * RET_DIM
OFF_GA = OFF_GLA + 2 * GLA_KEY_DIM + 2 * GLA_VAL_DIM
OFF_GATES = OFF_GA + GLA_GATE_RANK
GLA_COLS = 2 * GLA_KEY_DIM + 2 * GLA_VAL_DIM

LANES = 128
W_GATES = OFF_GA
W_GA = W_GATES + N_BRANCHES * D_MODEL
W_COLS = W_GA + LANES
SUBLANES = 8
MXU_COLS = 256
GLA_CHUNK = 64
GLA_SUB = 8
DIAG_TILE = 16
VMEM_LIMIT_BYTES = 56 * 1024 * 1024
MIXER_BLOCK = 256
FFN_BLOCK = 1024
ROPE_BLOCK = 2048

BF16 = jnp.bfloat16
F32 = jnp.float32


def _dot(a, b):
    return jnp.dot(a, b, preferred_element_type=F32)


def _dot_nt(a, b):
    return lax.dot_general(a, b, (((1,), (1,)), ((), ())), preferred_element_type=F32)


def _dot_tn(a, b):
    return lax.dot_general(a, b, (((0,), (0,)), ((), ())), preferred_element_type=F32)


def _rms(x, w):
    return x * lax.rsqrt(jnp.mean(x * x, axis=-1, keepdims=True) + EPS) * w


def _sigmoid(x):
    return 1.0 / (1.0 + jnp.exp2(x * -LOG2E))


def _silu(x):
    return x * _sigmoid(x)


def _log_sigmoid(x):
    return jnp.minimum(x, 0.0) - jnp.log(1.0 + jnp.exp2(jnp.abs(x) * -LOG2E))


def _rope_kernel(pos_ref, invf_ref, cos_ref, sin_ref):
    tb = pos_ref.shape[1]
    hb = tb // 2
    half = RET_HEAD_DIM // 2
    pos = pos_ref[0].astype(F32)
    lo = lax.broadcasted_iota(jnp.int32, (hb, RET_HEAD_DIM), 1) < half
    ang = jnp.where(lo, pos[0:hb], pos[hb:tb]) * invf_ref[...]
    c = jnp.cos(ang)
    s = jnp.sin(ang)
    c_sw = pltpu.roll(c, half, axis=1)
    s_sw = pltpu.roll(s, half, axis=1)
    sign = jnp.where(lo, -1.0, 1.0)
    cos_ref[0, 0:hb, :] = jnp.where(lo, c, c_sw)
    cos_ref[0, hb:tb, :] = jnp.where(lo, c_sw, c)
    sin_ref[0, 0:hb, :] = jnp.where(lo, s, s_sw) * sign
    sin_ref[0, hb:tb, :] = jnp.where(lo, s_sw, s) * sign


def _rope_tables(positions, tb):
    b, t = positions.shape
    half = RET_HEAD_DIM // 2
    inv_freq = ROPE_BASE ** (-jnp.arange(half, dtype=F32) / half)
    invf = jnp.concatenate([inv_freq, inv_freq]).reshape(1, RET_HEAD_DIM)
    pos3 = positions.reshape(b, t, 1)
    out = jax.ShapeDtypeStruct((b, t, RET_HEAD_DIM), F32)
    return pl.pallas_call(
        _rope_kernel,
        grid=(b, t // tb),
        in_specs=[pl.BlockSpec((1, tb, 1), lambda i, j: (i, j, 0)),
                  pl.BlockSpec((1, RET_HEAD_DIM), lambda i, j: (0, 0))],
        out_specs=[pl.BlockSpec((1, tb, RET_HEAD_DIM), lambda i, j: (i, j, 0)),
                   pl.BlockSpec((1, tb, RET_HEAD_DIM), lambda i, j: (i, j, 0))],
        out_shape=[out, out],
        name="rope_tables",
    )(pos3, invf)


def _mixer_kernel(x_ref, xnext_ref, cos_ref, sin_ref, npre_ref, win_ref, convw_ref,
                  retgn_ref, wa2_ref, ba_ref, glagn_ref, wba_ref, wbb_ref, wbc_ref, wout_ref,
                  npost_ref, emat_ref, out_ref,
                  h_s, hn_s, pa_s, pb_s, pc_s, gate_s, ob_s, oc_s, m_s, zc_s, rstate_s, gstate_s, cum_s,
                  td_s, srep_s, dmask_s, rdec_s, *, tb):
    b_idx = pl.program_id(0)
    t_idx = pl.program_id(1)

    @pl.when((b_idx == 0) & (t_idx == 0))
    def _():
        ri = lax.broadcasted_iota(jnp.int32, (tb, tb), 0)
        ci = lax.broadcasted_iota(jnp.int32, (tb, tb), 1)
        dif = jnp.maximum(ri - ci, 0).astype(F32)
        rowf = lax.broadcasted_iota(jnp.int32, (tb, RET_HEAD_DIM), 0).astype(F32)
        for hh in range(RET_HEADS):
            log_g = math.log(1.0 - 2.0 ** (-5.0 - hh))
            dmask_s[hh] = jnp.where(ri >= ci, jnp.exp(log_g * dif), 0.0)
            rdec_s[2 * hh] = jnp.exp(log_g * (rowf + 1.0))
            rdec_s[2 * hh + 1] = jnp.exp(log_g * (tb - 1.0 - rowf))

    @pl.when(t_idx == 0)
    def _():
        zc_s[0:SUBLANES, :] = jnp.zeros((SUBLANES, CONV_DIM), F32)
        rstate_s[...] = jnp.zeros(rstate_s.shape, F32)
        gstate_s[...] = jnp.zeros(gstate_s.shape, F32)

    def proj_task(dst_ref, w_ref, w_lo, lo, act=None, h_ref=h_s):
        def run():
            r = _dot(h_ref[...], w_ref[:, w_lo + lo:w_lo + lo + MXU_COLS])
            dst_ref[:, lo:lo + MXU_COLS] = r if act is None else act(r)
        return run

    ret_q_cols = range(0, RET_DIM, MXU_COLS)

    @pl.when((b_idx == 0) & (t_idx == 0))
    def _():
        h_s[...] = _rms(x_ref[0], npre_ref[...]).astype(BF16)
        for lo in ret_q_cols:
            proj_task(pb_s, win_ref, OFF_RET, lo)()

    half_tb = tb // 2

    def next_h_task(r0):
        def run():
            xn = xnext_ref[0, r0:r0 + half_tb, :]
            hn_s[r0:r0 + half_tb, :] = _rms(xn, npre_ref[...]).astype(BF16)
        return run

    def interleave(primary, fillers):
        done = 0
        for i, step in enumerate(primary):
            step()
            while done * len(primary) < (i + 1) * len(fillers):
                fillers[done]()
                done += 1

    ret_proj = [proj_task(pb_s, win_ref, OFF_RET, lo) for lo in range(RET_DIM, 4 * RET_DIM, MXU_COLS)]
    conv_proj = [proj_task(pa_s, win_ref, OFF_CONV, lo) for lo in range(0, 3 * CONV_DIM, MXU_COLS)]
    gate_proj = [proj_task(gate_s, win_ref, W_GATES, lo, _sigmoid) for lo in range(0, N_BRANCHES * D_MODEL, MXU_COLS)]

    gc = GLA_CHUNK
    kd = GLA_KEY_DIM
    qscale = GLA_KEY_HEAD_DIM ** -0.5
    n_chunks = tb // gc
    gla_proj = [proj_task(pc_s, win_ref, OFF_GLA, lo) for lo in range(0, GLA_COLS, MXU_COLS)]
    ga_down = _dot(h_s[...], win_ref[:, W_GA:W_COLS])
    logits = _dot(ga_down.astype(BF16), wa2_ref[...]) + ba_ref[...]
    for task in gla_proj[:3]:
        task()
    log_a = _log_sigmoid(logits) * (LOG2E / GLA_GATE_TAU)
    ri = lax.broadcasted_iota(jnp.int32, (gc, gc), 0)
    ci = lax.broadcasted_iota(jnp.int32, (gc, gc), 1)
    tril = jnp.where(ri >= ci, 1.0, 0.0).astype(BF16)
    for c in range(n_chunks):
        g = log_a[c * gc:(c + 1) * gc, :]
        g_hi = g.astype(BF16)
        g_lo = (g - g_hi.astype(F32)).astype(BF16)
        cum_s[c * gc:(c + 1) * gc, :] = _dot(tril, g_hi) + _dot(tril, g_lo)
    for task in gla_proj[3:]:
        task()

    def diag_unit(u):
        r0 = u * DIAG_TILE
        subs = []
        for rs in range(r0, r0 + DIAG_TILE, GLA_SUB):
            subs.append((rs, pc_s[rs:rs + GLA_SUB, 0:kd] * qscale, cum_s[rs:rs + GLA_SUB, :]))
        for j in range(GLA_SUB):
            tjs = []
            for rs, qb, cb in subs:
                kj = pc_s[rs + j:rs + j + 1, kd:2 * kd]
                cj = cum_s[rs + j:rs + j + 1, :]
                tjs.append(qb * kj * jnp.exp2(jnp.minimum(cb - cj, 0.0)))
            td_s[r0:r0 + DIAG_TILE, j * kd:(j + 1) * kd] = jnp.concatenate(tjs, axis=0).astype(BF16)

    interleave([functools.partial(diag_unit, u) for u in range(tb // DIAG_TILE)], ret_proj + conv_proj[:4])

    def conv_task():
        z = pa_s[:, 2 * CONV_DIM:3 * CONV_DIM] * pa_s[:, 0:CONV_DIM]
        zc_s[SUBLANES:SUBLANES + tb, :] = z
        z1 = zc_s[SUBLANES - 1:SUBLANES - 1 + tb, :]
        z2 = zc_s[SUBLANES - 2:SUBLANES - 2 + tb, :]
        cw = convw_ref[...]
        y_a = pa_s[:, CONV_DIM:2 * CONV_DIM] * (cw[0:1, :] * z2 + cw[1:2, :] * z1 + cw[2:3, :] * z)
        zc_s[0:SUBLANES, :] = z[tb - SUBLANES:tb, :]
        m_s[...] = gate_s[:, 0:D_MODEL] * _dot(y_a.astype(BF16), wba_ref[...])

    ret = [dict() for _ in range(RET_HEADS)]

    def ret_rotary(hh):
        lo = hh * RET_HEAD_DIM
        cosf = cos_ref[0]
        sins = sin_ref[0]
        qh = pb_s[:, lo:lo + RET_HEAD_DIM]
        kh = pb_s[:, RET_DIM + lo:RET_DIM + lo + RET_HEAD_DIM]
        qc = (qh * cosf + pltpu.roll(qh, RET_HEAD_DIM // 2, axis=1) * sins) * (RET_HEAD_DIM ** -0.5)
        kc = kh * cosf + pltpu.roll(kh, RET_HEAD_DIM // 2, axis=1) * sins
        ret[hh].update(q=qc.astype(BF16), k=kc.astype(BF16),
                       qd=(qc * rdec_s[2 * hh]).astype(BF16), kd=(kc * rdec_s[2 * hh + 1]).astype(BF16),
                       v=pb_s[:, 2 * RET_DIM + lo:2 * RET_DIM + lo + RET_HEAD_DIM].astype(BF16))

    def ret_scores(hh):
        r = ret[hh]
        log_g = math.log(1.0 - 2.0 ** (-5.0 - hh))
        state = rstate_s[hh]
        r["s"] = (_dot_nt(r["q"], r["k"]) * dmask_s[hh]).astype(BF16)
        r["inter"] = _dot(r["qd"], state.astype(BF16))
        rstate_s[hh] = state * math.exp(log_g * tb) + _dot_tn(r["kd"], r["v"])

    def ret_output(hh):
        r = ret[hh]
        lo = hh * RET_HEAD_DIM
        o = _dot(r["s"], r["v"]) + r["inter"]
        oc = o - jnp.mean(o, axis=-1, keepdims=True)
        ob_s[:, lo:lo + RET_HEAD_DIM] = oc * lax.rsqrt(jnp.mean(oc * oc, axis=-1, keepdims=True) + EPS)

    rowi = lax.broadcasted_iota(jnp.int32, (gc, LANES), 0)
    lanei = lax.broadcasted_iota(jnp.int32, (gc, LANES), 1)
    colj = lanei & (GLA_KEY_HEAD_DIM - 1)
    sub_shift = GLA_SUB.bit_length() - 1
    diag_mask = ((colj >> sub_shift) == (rowi >> sub_shift)) & (colj <= rowi)
    head_lo = lanei < GLA_KEY_HEAD_DIM
    srow = lax.broadcasted_iota(jnp.int32, (2 * GLA_VAL_HEAD_DIM, LANES), 0)
    slane = lax.broadcasted_iota(jnp.int32, (2 * GLA_VAL_HEAD_DIM, LANES), 1)
    state_mask = (srow < GLA_VAL_HEAD_DIM) == (slane < GLA_KEY_HEAD_DIM)
    zero_v = jnp.zeros((gc, GLA_VAL_HEAD_DIM), BF16)
    block_masks = {}
    half = gc // 4
    while half >= GLA_SUB:
        shift = (2 * half).bit_length() - 1
        block_masks[half] = (colj >> shift) == (rowi >> shift)
        half //= 2

    gla = [[dict() for _ in range(GLA_HEADS // 2)] for _ in range(n_chunks)]

    def gla_intra(c):
        rows = slice(c * gc, (c + 1) * gc)
        cum = cum_s[rows, :]
        last = cum[gc - 1:gc, :]
        q = pc_s[rows, 0:kd] * qscale
        k = pc_s[rows, kd:2 * kd]
        q_in = q * jnp.exp2(cum)
        k_st = k * jnp.exp2(last - cum)
        levels = []
        half = gc // 2
        while half >= GLA_SUB:
            a_rows, b_rows = [], []
            zero_half = jnp.zeros((half, kd), F32)
            for base in range(0, gc, 2 * half):
                mid = base + half
                ref = cum[mid - 1:mid, :]
                b_rows += [k[base:mid, :] * jnp.exp2(ref - cum[base:mid, :]), zero_half]
                a_rows += [zero_half, q[mid:mid + half, :] * jnp.exp2(cum[mid:mid + half, :] - ref)]
            levels.append((half, jnp.concatenate(a_rows, axis=0), jnp.concatenate(b_rows, axis=0)))
            half //= 2
        for pr in range(GLA_HEADS // 2):
            ls = slice(pr * LANES, (pr + 1) * LANES)
            s_off = None
            for half, a, b in levels:
                b_two = jnp.concatenate([jnp.where(head_lo, b[:, ls], 0.0).astype(BF16),
                                         jnp.where(head_lo, 0.0, b[:, ls]).astype(BF16)], axis=0)
                r = _dot_nt(a[:, ls].astype(BF16), b_two)
                if 2 * half < gc:
                    r = jnp.where(block_masks[half], r, 0.0)
                s_off = r if s_off is None else s_off + r
            s = jnp.where(diag_mask, srep_s[rows, ls], s_off).astype(BF16)
            v_lo = 2 * kd + 2 * pr * GLA_VAL_HEAD_DIM
            v0 = pc_s[rows, v_lo:v_lo + GLA_VAL_HEAD_DIM].astype(BF16)
            v1 = pc_s[rows, v_lo + GLA_VAL_HEAD_DIM:v_lo + 2 * GLA_VAL_HEAD_DIM].astype(BF16)
            v_bd = jnp.concatenate([jnp.concatenate([v0, zero_v], axis=1),
                                    jnp.concatenate([zero_v, v1], axis=1)], axis=0)
            v_pair = jnp.concatenate([v0, v1], axis=1)
            upd = _dot_tn(v_pair, k_st[:, ls].astype(BF16))
            gla[c][pr].update(intra=_dot(s, v_bd), q_in=q_in[:, ls].astype(BF16),
                              upd=jnp.where(state_mask, upd, 0.0), decay=jnp.exp2(last[:, ls]))

    def gla_carry():
        for pr in range(GLA_HEADS // 2):
            st = gstate_s[pr]
            for c in range(n_chunks):
                g = gla[c][pr]
                o = g["intra"] + _dot_nt(g["q_in"], st.astype(BF16))
                st = st * g["decay"] + g["upd"]
                for hl in range(2):
                    oh = o[:, hl * GLA_VAL_HEAD_DIM:(hl + 1) * GLA_VAL_HEAD_DIM]
                    lo = (2 * pr + hl) * GLA_VAL_HEAD_DIM
                    oc_s[c * gc:(c + 1) * gc, lo:lo + GLA_VAL_HEAD_DIM] = (
                        oh * lax.rsqrt(jnp.mean(oh * oh, axis=-1, keepdims=True) + EPS))
            gstate_s[pr] = st

    def srep_task():
        srep_s[...] = _dot(td_s[...], emat_ref[...])

    heads = range(RET_HEADS)
    interleave([next_h_task(0), next_h_task(half_tb)] + [functools.partial(ret_rotary, hh) for hh in heads],
               conv_proj[4:])
    interleave([srep_task] + [functools.partial(ret_scores, hh) for hh in heads], gate_proj[:4])
    interleave([functools.partial(gla_intra, c) for c in range(n_chunks)], gate_proj[4:8])
    interleave([functools.partial(ret_output, hh) for hh in heads], gate_proj[8:])
    h_s[...] = hn_s[...]
    interleave([gla_carry], [conv_task])

    y_b = ob_s[...] * retgn_ref[...] * _silu(pb_s[:, 3 * RET_DIM:4 * RET_DIM])
    m_s[...] += gate_s[:, D_MODEL:2 * D_MODEL] * _dot(y_b.astype(BF16), wbb_ref[...])
    y_c = oc_s[...] * glagn_ref[...] * _silu(pc_s[:, 2 * kd + GLA_VAL_DIM:2 * kd + 2 * GLA_VAL_DIM])
    merged = m_s[...] + gate_s[:, 2 * D_MODEL:3 * D_MODEL] * _dot(y_c.astype(BF16), wbc_ref[...])

    merged_b = merged.astype(BF16)
    sumsq = jnp.zeros((tb, 1), F32)
    for lo in range(0, D_MODEL, MXU_COLS):
        blk = _dot(merged_b, wout_ref[:, lo:lo + MXU_COLS])
        m_s[:, lo:lo + MXU_COLS] = blk
        sumsq = sumsq + jnp.sum(blk * blk, axis=-1, keepdims=True)
    for lo in ret_q_cols:
        proj_task(pb_s, win_ref, OFF_RET, lo, h_ref=hn_s)()
    scale = lax.rsqrt(sumsq * (1.0 / D_MODEL) + EPS)
    out_ref[0] = x_ref[0] + m_s[...] * scale * npost_ref[...]


def _layer_spec(arr, layer, grid_rank):
    index = (lambda i: (layer, 0, 0)) if grid_rank == 1 else (lambda i, j: (layer, 0, 0))
    return pl.BlockSpec((None,) + arr.shape[1:], index, pipeline_mode=pl.Buffered(1))


def _diag_reduce_matrix():
    kidx = np.arange(GLA_SUB * GLA_KEY_DIM)
    j = kidx // GLA_KEY_DIM
    hk = (kidx % GLA_KEY_DIM) // GLA_KEY_HEAD_DIM
    col = np.arange(GLA_KEY_DIM)
    hc = col // GLA_KEY_HEAD_DIM
    cj = (col % GLA_KEY_HEAD_DIM) % GLA_SUB
    return jnp.asarray((hk[:, None] == hc[None, :]) & (j[:, None] == cj[None, :]), dtype=BF16)


def _mixer(x, cos_t, sin_t, layer_params, emat, layer, tb):
    b, t, d = x.shape
    nt = t // tb
    tok = lambda width: pl.BlockSpec((1, tb, width), lambda i, j: (i, j, 0))
    next_tok = pl.BlockSpec((1, tb, d), lambda i, j: (jnp.minimum(i + (j + 1) // nt, b - 1), (j + 1) % nt, 0))
    const_specs = [_layer_spec(p, layer, 2) for p in layer_params]
    const_specs.append(pl.BlockSpec(emat.shape, lambda i, j: (0, 0), pipeline_mode=pl.Buffered(1)))
    scratch = [
        pltpu.VMEM((tb, D_MODEL), BF16),
        pltpu.VMEM((tb, D_MODEL), BF16),
        pltpu.VMEM((tb, 3 * CONV_DIM), F32),
        pltpu.VMEM((tb, 4 * RET_DIM), F32),
        pltpu.VMEM((tb, GLA_COLS), F32),
        pltpu.VMEM((tb, N_BRANCHES * D_MODEL), F32),
        pltpu.VMEM((tb, RET_DIM), F32),
        pltpu.VMEM((tb, GLA_VAL_DIM), F32),
        pltpu.VMEM((tb, D_MODEL), F32),
        pltpu.VMEM((tb + SUBLANES, CONV_DIM), F32),
        pltpu.VMEM((RET_HEADS, RET_HEAD_DIM, RET_HEAD_DIM), F32),
        pltpu.VMEM((GLA_HEADS // 2, 2 * GLA_VAL_HEAD_DIM, LANES), F32),
        pltpu.VMEM((tb, GLA_KEY_DIM), F32),
        pltpu.VMEM((tb, GLA_SUB * GLA_KEY_DIM), BF16),
        pltpu.VMEM((tb, GLA_KEY_DIM), F32),
        pltpu.VMEM((RET_HEADS, tb, tb), F32),
        pltpu.VMEM((2 * RET_HEADS, tb, RET_HEAD_DIM), F32),
    ]
    return pl.pallas_call(
        functools.partial(_mixer_kernel, tb=tb),
        grid=(b, t // tb),
        in_specs=[tok(d), next_tok, tok(RET_HEAD_DIM), tok(RET_HEAD_DIM)] + const_specs,
        out_specs=tok(d),
        out_shape=jax.ShapeDtypeStruct(x.shape, x.dtype),
        scratch_shapes=scratch,
        compiler_params=pltpu.CompilerParams(dimension_semantics=("arbitrary", "arbitrary"),
                                             vmem_limit_bytes=VMEM_LIMIT_BYTES),
        name="mixer",
    )(x, x, cos_t, sin_t, *layer_params, emat)


def _ffn_kernel(x_ref, npre_ref, wg_ref, wu_ref, wd_ref, npost_ref, out_ref, *, fc):
    x = x_ref[...]
    h = _rms(x, npre_ref[...]).astype(BF16)
    acc = jnp.zeros(x.shape, F32)
    for f in range(0, D_FF, fc):
        g = _dot(h, wg_ref[:, f:f + fc])
        u = _dot(h, wu_ref[:, f:f + fc])
        acc = acc + _dot((_silu(g) * u).astype(BF16), wd_ref[f:f + fc, :])
    out_ref[...] = x + _rms(acc, npost_ref[...])


def _ffn(x2, layer_params, layer, tm, fc):
    n, d = x2.shape
    return pl.pallas_call(
        functools.partial(_ffn_kernel, fc=fc),
        grid=(n // tm,),
        in_specs=[pl.BlockSpec((tm, d), lambda i: (i, 0))] + [_layer_spec(p, layer, 1) for p in layer_params],
        out_specs=pl.BlockSpec((tm, d), lambda i: (i, 0)),
        out_shape=jax.ShapeDtypeStruct(x2.shape, x2.dtype),
        compiler_params=pltpu.CompilerParams(dimension_semantics=("arbitrary",),
                                             vmem_limit_bytes=VMEM_LIMIT_BYTES),
        name="swiglu",
    )(x2, *layer_params)


def _pick_block(t, want):
    tb = min(want, t)
    while t % tb:
        tb //= 2
    return tb


def kernel(x, positions, norm_mix_pre, w_in, conv_w, ret_gn_w, gla_w_a2, gla_b_a, gla_gn_w, w_branch_a, w_branch_b, w_branch_c, w_out, norm_mix_post, norm_ffn_pre, w_ffn_gate, w_ffn_up, w_ffn_down, norm_ffn_post):
    b, t, d = x.shape
    depth = w_in.shape[0]
    tb = _pick_block(t, MIXER_BLOCK)
    tm = _pick_block(b * t, FFN_BLOCK)
    cos_t, sin_t = _rope_tables(positions, _pick_block(t, ROPE_BLOCK))
    emat = _diag_reduce_matrix()
    rows = lambda v: v.reshape(depth, 1, -1)
    pad_rank = LANES - GLA_GATE_RANK
    w_packed = jnp.concatenate(
        [w_in[:, :, :OFF_GA], w_in[:, :, OFF_GATES:],
         jnp.pad(w_in[:, :, OFF_GA:OFF_GATES], ((0, 0), (0, 0), (0, pad_rank)))], axis=2).astype(BF16)
    wa2 = jnp.pad(gla_w_a2, ((0, 0), (0, pad_rank), (0, 0))).astype(BF16)
    mixer_params = [rows(norm_mix_pre), w_packed, conv_w, rows(ret_gn_w), wa2, rows(gla_b_a), rows(gla_gn_w),
                    w_branch_a.astype(BF16), w_branch_b.astype(BF16), w_branch_c.astype(BF16),
                    w_out.astype(BF16), rows(norm_mix_post)]
    ffn_params = [rows(norm_ffn_pre), w_ffn_gate.astype(BF16), w_ffn_up.astype(BF16),
                  w_ffn_down.astype(BF16), rows(norm_ffn_post)]
    for l in range(depth):
        x = _mixer(x, cos_t, sin_t, mixer_params, emat, l, tb)
        x = _ffn(x.reshape(b * t, d), ffn_params, l, tm, MXU_COLS).reshape(b, t, d)
    return x
```

```python
import functools
import math

import jax
import jax.numpy as jnp
import numpy as np
from jax import lax
from jax.experimental import pallas as pl
from jax.experimental.pallas import tpu as pltpu

D_MODEL = 1024
CONV_DIM = 512
CONV_WIDTH = 3
RET_HEADS = 4
RET_HEAD_DIM = 128
RET_DIM = RET_HEADS * RET_HEAD_DIM
GLA_HEADS = 4
GLA_KEY_HEAD_DIM = 64
GLA_VAL_HEAD_DIM = 128
GLA_KEY_DIM = GLA_HEADS * GLA_KEY_HEAD_DIM
GLA_VAL_DIM = GLA_HEADS * GLA_VAL_HEAD_DIM
GLA_GATE_RANK = 16
GLA_GATE_TAU = 16.0
N_BRANCHES = 3
D_FF = 2816
ROPE_BASE = 10000.0
EPS = 1e-6
LOG2E = 1.0 / math.log(2.0)

OFF_CONV = 0
OFF_RET = 3 * CONV_DIM
OFF_GLA = OFF_RET + 4 * RET_DIM
OFF_GA = OFF_GLA + 2 * GLA_KEY_DIM + 2 * GLA_VAL_DIM
OFF_GATES = OFF_GA + GLA_GATE_RANK
GLA_COLS = 2 * GLA_KEY_DIM + 2 * GLA_VAL_DIM

LANES = 128
W_GATES = OFF_GA
W_GA = W_GATES + N_BRANCHES * D_MODEL
W_COLS = W_GA + LANES
SUBLANES = 8
MXU_COLS = 256
GLA_CHUNK = 64
GLA_SUB = 8
DIAG_TILE = 16
VMEM_LIMIT_BYTES = 56 * 1024 * 1024
MIXER_BLOCK = 256
FFN_BLOCK = 1024
ROPE_BLOCK = 2048

BF16 = jnp.bfloat16
F32 = jnp.float32


def _dot(a, b):
    return jnp.dot(a, b, preferred_element_type=F32)


def _dot_nt(a, b):
    return lax.dot_general(a, b, (((1,), (1,)), ((), ())), preferred_element_type=F32)


def _dot_tn(a, b):
    return lax.dot_general(a, b, (((0,), (0,)), ((), ())), preferred_element_type=F32)


def _rms(x, w):
    return x * lax.rsqrt(jnp.mean(x * x, axis=-1, keepdims=True) + EPS) * w


def _sigmoid(x):
    return 1.0 / (1.0 + jnp.exp2(x * -LOG2E))


def _silu(x):
    return x * _sigmoid(x)


def _log_sigmoid(x):
    return jnp.minimum(x, 0.0) - jnp.log(1.0 + jnp.exp2(jnp.abs(x) * -LOG2E))


def _rope_kernel(pos_ref, invf_ref, cos_ref, sin_ref):
    tb = pos_ref.shape[1]
    hb = tb // 2
    half = RET_HEAD_DIM // 2
    pos = pos_ref[0].astype(F32)
    lo = lax.broadcasted_iota(jnp.int32, (hb, RET_HEAD_DIM), 1) < half
    ang = jnp.where(lo, pos[0:hb], pos[hb:tb]) * invf_ref[...]
    c = jnp.cos(ang)
    s = jnp.sin(ang)
    c_sw = pltpu.roll(c, half, axis=1)
    s_sw = pltpu.roll(s, half, axis=1)
    sign = jnp.where(lo, -1.0, 1.0)
    cos_ref[0, 0:hb, :] = jnp.where(lo, c, c_sw)
    cos_ref[0, hb:tb, :] = jnp.where(lo, c_sw, c)
    sin_ref[0, 0:hb, :] = jnp.where(lo, s, s_sw) * sign
    sin_ref[0, hb:tb, :] = jnp.where(lo, s_sw, s) * sign


def _rope_tables(positions, tb):
    b, t = positions.shape
    half = RET_HEAD_DIM // 2
    inv_freq = ROPE_BASE ** (-jnp.arange(half, dtype=F32) / half)
    invf = jnp.concatenate([inv_freq, inv_freq]).reshape(1, RET_HEAD_DIM)
    pos3 = positions.reshape(b, t, 1)
    out = jax.ShapeDtypeStruct((b, t, RET_HEAD_DIM), F32)
    return pl.pallas_call(
        _rope_kernel,
        grid=(b, t // tb),
        in_specs=[pl.BlockSpec((1, tb, 1), lambda i, j: (i, j, 0)),
                  pl.BlockSpec((1, RET_HEAD_DIM), lambda i, j: (0, 0))],
        out_specs=[pl.BlockSpec((1, tb, RET_HEAD_DIM), lambda i, j: (i, j, 0)),
                   pl.BlockSpec((1, tb, RET_HEAD_DIM), lambda i, j: (i, j, 0))],
        out_shape=[out, out],
        name="rope_tables",
    )(pos3, invf)


def _mixer_kernel(x_ref, xnext_ref, cos_ref, sin_ref, npre_ref, win_ref, convw_ref,
                  retgn_ref, wa2_ref, ba_ref, glagn_ref, wba_ref, wbb_ref, wbc_ref, wout_ref,
                  npost_ref, emat_ref, wg32_ref, wu32_ref, wd32_ref, out_ref, wg16_ref, wu16_ref, wd16_ref,
                  h_s, hn_s, pa_s, pb_s, pc_s, gate_s, ob_s, oc_s, m_s, zc_s, rstate_s, gstate_s, cum_s,
                  td_s, srep_s, dmask_s, rdec_s, *, tb):
    b_idx = pl.program_id(0)
    t_idx = pl.program_id(1)

    @pl.when((b_idx == 0) & (t_idx == 0))
    def _():
        ri = lax.broadcasted_iota(jnp.int32, (tb, tb), 0)
        ci = lax.broadcasted_iota(jnp.int32, (tb, tb), 1)
        dif = jnp.maximum(ri - ci, 0).astype(F32)
        rowf = lax.broadcasted_iota(jnp.int32, (tb, RET_HEAD_DIM), 0).astype(F32)
        for hh in range(RET_HEADS):
            log_g = math.log(1.0 - 2.0 ** (-5.0 - hh))
            dmask_s[hh] = jnp.where(ri >= ci, jnp.exp(log_g * dif), 0.0)
            rdec_s[2 * hh] = jnp.exp(log_g * (rowf + 1.0))
            rdec_s[2 * hh + 1] = jnp.exp(log_g * (tb - 1.0 - rowf))

    @pl.when(t_idx == 0)
    def _():
        zc_s[0:SUBLANES, :] = jnp.zeros((SUBLANES, CONV_DIM), F32)
        rstate_s[...] = jnp.zeros(rstate_s.shape, F32)
        gstate_s[...] = jnp.zeros(gstate_s.shape, F32)

    def proj_task(dst_ref, w_ref, w_lo, lo, act=None, h_ref=h_s):
        def run():
            r = _dot(h_ref[...], w_ref[:, w_lo + lo:w_lo + lo + MXU_COLS])
            dst_ref[:, lo:lo + MXU_COLS] = r if act is None else act(r)
        return run

    ret_q_cols = range(0, RET_DIM, MXU_COLS)

    @pl.when((b_idx == 0) & (t_idx == 0))
    def _():
        h_s[...] = _rms(x_ref[0], npre_ref[...]).astype(BF16)
        for lo in ret_q_cols:
            proj_task(pb_s, win_ref, OFF_RET, lo)()

    half_tb = tb // 2

    def next_h_task(r0):
        def run():
            xn = xnext_ref[0, r0:r0 + half_tb, :]
            hn_s[r0:r0 + half_tb, :] = _rms(xn, npre_ref[...]).astype(BF16)
        return run

    def ffn_cast_task():
        wg16_ref[...] = wg32_ref[...].astype(BF16)
        wu16_ref[...] = wu32_ref[...].astype(BF16)
        wd16_ref[...] = wd32_ref[...].astype(BF16)

    def interleave(primary, fillers):
        done = 0
        for i, step in enumerate(primary):
            step()
            while done * len(primary) < (i + 1) * len(fillers):
                fillers[done]()
                done += 1

    ret_proj = [proj_task(pb_s, win_ref, OFF_RET, lo) for lo in range(RET_DIM, 4 * RET_DIM, MXU_COLS)]
    conv_proj = [proj_task(pa_s, win_ref, OFF_CONV, lo) for lo in range(0, 3 * CONV_DIM, MXU_COLS)]
    gate_proj = [proj_task(gate_s, win_ref, W_GATES, lo, _sigmoid) for lo in range(0, N_BRANCHES * D_MODEL, MXU_COLS)]

    gc = GLA_CHUNK
    kd = GLA_KEY_DIM
    qscale = GLA_KEY_HEAD_DIM ** -0.5
    n_chunks = tb // gc
    gla_proj = [proj_task(pc_s, win_ref, OFF_GLA, lo) for lo in range(0, GLA_COLS, MXU_COLS)]
    ga_down = _dot(h_s[...], win_ref[:, W_GA:W_COLS])
    logits = _dot(ga_down.astype(BF16), wa2_ref[...]) + ba_ref[...]
    for task in gla_proj[:3]:
        task()
    log_a = _log_sigmoid(logits) * (LOG2E / GLA_GATE_TAU)
    ri = lax.broadcasted_iota(jnp.int32, (gc, gc), 0)
    ci = lax.broadcasted_iota(jnp.int32, (gc, gc), 1)
    tril = jnp.where(ri >= ci, 1.0, 0.0).astype(BF16)
    for c in range(n_chunks):
        g = log_a[c * gc:(c + 1) * gc, :]
        g_hi = g.astype(BF16)
        g_lo = (g - g_hi.astype(F32)).astype(BF16)
        cum_s[c * gc:(c + 1) * gc, :] = _dot(tril, g_hi) + _dot(tril, g_lo)
    for task in gla_proj[3:]:
        task()

    def diag_unit(u):
        r0 = u * DIAG_TILE
        subs = []
        for rs in range(r0, r0 + DIAG_TILE, GLA_SUB):
            subs.append((rs, pc_s[rs:rs + GLA_SUB, 0:kd] * qscale, cum_s[rs:rs + GLA_SUB, :]))
        for j in range(GLA_SUB):
            tjs = []
            for rs, qb, cb in subs:
                kj = pc_s[rs + j:rs + j + 1, kd:2 * kd]
                cj = cum_s[rs + j:rs + j + 1, :]
                tjs.append(qb * kj * jnp.exp2(jnp.minimum(cb - cj, 0.0)))
            td_s[r0:r0 + DIAG_TILE, j * kd:(j + 1) * kd] = jnp.concatenate(tjs, axis=0).astype(BF16)

    interleave([functools.partial(diag_unit, u) for u in range(tb // DIAG_TILE)], ret_proj + conv_proj[:4])

    def conv_task():
        z = pa_s[:, 2 * CONV_DIM:3 * CONV_DIM] * pa_s[:, 0:CONV_DIM]
        zc_s[SUBLANES:SUBLANES + tb, :] = z
        z1 = zc_s[SUBLANES - 1:SUBLANES - 1 + tb, :]
        z2 = zc_s[SUBLANES - 2:SUBLANES - 2 + tb, :]
        cw = convw_ref[...]
        y_a = pa_s[:, CONV_DIM:2 * CONV_DIM] * (cw[0:1, :] * z2 + cw[1:2, :] * z1 + cw[2:3, :] * z)
        zc_s[0:SUBLANES, :] = z[tb - SUBLANES:tb, :]
        m_s[...] = gate_s[:, 0:D_MODEL] * _dot(y_a.astype(BF16), wba_ref[...])

    ret = [dict() for _ in range(RET_HEADS)]

    def ret_rotary(hh):
        lo = hh * RET_HEAD_DIM
        cosf = cos_ref[0]
        sins = sin_ref[0]
        qh = pb_s[:, lo:lo + RET_HEAD_DIM]
        kh = pb_s[:, RET_DIM + lo:RET_DIM + lo + RET_HEAD_DIM]
        qc = (qh * cosf + pltpu.roll(qh, RET_HEAD_DIM // 2, axis=1) * sins) * (RET_HEAD_DIM ** -0.5)
        kc = kh * cosf + pltpu.roll(kh, RET_HEAD_DIM // 2, axis=1) * sins
        ret[hh].update(q=qc.astype(BF16), k=kc.astype(BF16),
                       qd=(qc * rdec_s[2 * hh]).astype(BF16), kd=(kc * rdec_s[2 * hh + 1]).astype(BF16),
                       v=pb_s[:, 2 * RET_DIM + lo:2 * RET_DIM + lo + RET_HEAD_DIM].astype(BF16))

    def ret_scores(hh):
        r = ret[hh]
        log_g = math.log(1.0 - 2.0 ** (-5.0 - hh))
        state = rstate_s[hh]
        r["s"] = (_dot_nt(r["q"], r["k"]) * dmask_s[hh]).astype(BF16)
        r["inter"] = _dot(r["qd"], state.astype(BF16))
        rstate_s[hh] = state * math.exp(log_g * tb) + _dot_tn(r["kd"], r["v"])

    def ret_output(hh):
        r = ret[hh]
        lo = hh * RET_HEAD_DIM
        o = _dot(r["s"], r["v"]) + r["inter"]
        oc = o - jnp.mean(o, axis=-1, keepdims=True)
        ob_s[:, lo:lo + RET_HEAD_DIM] = oc * lax.rsqrt(jnp.mean(oc * oc, axis=-1, keepdims=True) + EPS)

    rowi = lax.broadcasted_iota(jnp.int32, (gc, LANES), 0)
    lanei = lax.broadcasted_iota(jnp.int32, (gc, LANES), 1)
    colj = lanei & (GLA_KEY_HEAD_DIM - 1)
    sub_shift = GLA_SUB.bit_length() - 1
    diag_mask = ((colj >> sub_shift) == (rowi >> sub_shift)) & (colj <= rowi)
    head_lo = lanei < GLA_KEY_HEAD_DIM
    srow = lax.broadcasted_iota(jnp.int32, (2 * GLA_VAL_HEAD_DIM, LANES), 0)
    slane = lax.broadcasted_iota(jnp.int32, (2 * GLA_VAL_HEAD_DIM, LANES), 1)
    state_mask = (srow < GLA_VAL_HEAD_DIM) == (slane < GLA_KEY_HEAD_DIM)
    zero_v = jnp.zeros((gc, GLA_VAL_HEAD_DIM), BF16)
    block_masks = {}
    half = gc // 4
    while half >= GLA_SUB:
        shift = (2 * half).bit_length() - 1
        block_masks[half] = (colj >> shift) == (rowi >> shift)
        half //= 2

    gla = [[dict() for _ in range(GLA_HEADS // 2)] for _ in range(n_chunks)]

    def gla_intra(c):
        rows = slice(c * gc, (c + 1) * gc)
        cum = cum_s[rows, :]
        last = cum[gc - 1:gc, :]
        q = pc_s[rows, 0:kd] * qscale
        k = pc_s[rows, kd:2 * kd]
        q_in = q * jnp.exp2(cum)
        k_st = k * jnp.exp2(last - cum)
        levels = []
        half = gc // 2
        while half >= GLA_SUB:
            a_rows, b_rows = [], []
            zero_half = jnp.zeros((half, kd), F32)
            for base in range(0, gc, 2 * half):
                mid = base + half
                ref = cum[mid - 1:mid, :]
                b_rows += [k[base:mid, :] * jnp.exp2(ref - cum[base:mid, :]), zero_half]
                a_rows += [zero_half, q[mid:mid + half, :] * jnp.exp2(cum[mid:mid + half, :] - ref)]
            levels.append((half, jnp.concatenate(a_rows, axis=0), jnp.concatenate(b_rows, axis=0)))
            half //= 2
        for pr in range(GLA_HEADS // 2):
            ls = slice(pr * LANES, (pr + 1) * LANES)
            s_off = None
            for half, a, b in levels:
                b_two = jnp.concatenate([jnp.where(head_lo, b[:, ls], 0.0).astype(BF16),
                                         jnp.where(head_lo, 0.0, b[:, ls]).astype(BF16)], axis=0)
                r = _dot_nt(a[:, ls].astype(BF16), b_two)
                if 2 * half < gc:
                    r = jnp.where(block_masks[half], r, 0.0)
                s_off = r if s_off is None else s_off + r
            s = jnp.where(diag_mask, srep_s[rows, ls], s_off).astype(BF16)
            v_lo = 2 * kd + 2 * pr * GLA_VAL_HEAD_DIM
            v0 = pc_s[rows, v_lo:v_lo + GLA_VAL_HEAD_DIM].astype(BF16)
            v1 = pc_s[rows, v_lo + GLA_VAL_HEAD_DIM:v_lo + 2 * GLA_VAL_HEAD_DIM].astype(BF16)
            v_bd = jnp.concatenate([jnp.concatenate([v0, zero_v], axis=1),
                                    jnp.concatenate([zero_v, v1], axis=1)], axis=0)
            v_pair = jnp.concatenate([v0, v1], axis=1)
            upd = _dot_tn(v_pair, k_st[:, ls].astype(BF16))
            gla[c][pr].update(intra=_dot(s, v_bd), q_in=q_in[:, ls].astype(BF16),
                              upd=jnp.where(state_mask, upd, 0.0), decay=jnp.exp2(last[:, ls]))

    def gla_carry():
        for pr in range(GLA_HEADS // 2):
            st = gstate_s[pr]
            for c in range(n_chunks):
                g = gla[c][pr]
                o = g["intra"] + _dot_nt(g["q_in"], st.astype(BF16))
                st = st * g["decay"] + g["upd"]
                for hl in range(2):
                    oh = o[:, hl * GLA_VAL_HEAD_DIM:(hl + 1) * GLA_VAL_HEAD_DIM]
                    lo = (2 * pr + hl) * GLA_VAL_HEAD_DIM
                    oc_s[c * gc:(c + 1) * gc, lo:lo + GLA_VAL_HEAD_DIM] = (
                        oh * lax.rsqrt(jnp.mean(oh * oh, axis=-1, keepdims=True) + EPS))
            gstate_s[pr] = st

    def srep_task():
        srep_s[...] = _dot(td_s[...], emat_ref[...])

    heads = range(RET_HEADS)
    interleave([next_h_task(0), next_h_task(half_tb), ffn_cast_task]
               + [functools.partial(ret_rotary, hh) for hh in heads], conv_proj[4:])
    interleave([srep_task] + [functools.partial(ret_scores, hh) for hh in heads], gate_proj[:4])
    interleave([functools.partial(gla_intra, c) for c in range(n_chunks)], gate_proj[4:8])
    interleave([functools.partial(ret_output, hh) for hh in heads], gate_proj[8:])
    h_s[...] = hn_s[...]
    interleave([gla_carry], [conv_task])

    y_b = ob_s[...] * retgn_ref[...] * _silu(pb_s[:, 3 * RET_DIM:4 * RET_DIM])
    m_s[...] += gate_s[:, D_MODEL:2 * D_MODEL] * _dot(y_b.astype(BF16), wbb_ref[...])
    y_c = oc_s[...] * glagn_ref[...] * _silu(pc_s[:, 2 * kd + GLA_VAL_DIM:2 * kd + 2 * GLA_VAL_DIM])
    merged = m_s[...] + gate_s[:, 2 * D_MODEL:3 * D_MODEL] * _dot(y_c.astype(BF16), wbc_ref[...])

    merged_b = merged.astype(BF16)
    sumsq = jnp.zeros((tb, 1), F32)
    for lo in range(0, D_MODEL, MXU_COLS):
        blk = _dot(merged_b, wout_ref[:, lo:lo + MXU_COLS])
        m_s[:, lo:lo + MXU_COLS] = blk
        sumsq = sumsq + jnp.sum(blk * blk, axis=-1, keepdims=True)
    for lo in ret_q_cols:
        proj_task(pb_s, win_ref, OFF_RET, lo, h_ref=hn_s)()
    scale = lax.rsqrt(sumsq * (1.0 / D_MODEL) + EPS)
    out_ref[0] = x_ref[0] + m_s[...] * scale * npost_ref[...]


def _layer_spec(arr, layer, grid_rank):
    index = (lambda i: (layer, 0, 0)) if grid_rank == 1 else (lambda i, j: (layer, 0, 0))
    return pl.BlockSpec((None,) + arr.shape[1:], index, pipeline_mode=pl.Buffered(1))


def _diag_reduce_matrix():
    kidx = np.arange(GLA_SUB * GLA_KEY_DIM)
    j = kidx // GLA_KEY_DIM
    hk = (kidx % GLA_KEY_DIM) // GLA_KEY_HEAD_DIM
    col = np.arange(GLA_KEY_DIM)
    hc = col // GLA_KEY_HEAD_DIM
    cj = (col % GLA_KEY_HEAD_DIM) % GLA_SUB
    return jnp.asarray((hk[:, None] == hc[None, :]) & (j[:, None] == cj[None, :]), dtype=BF16)


def _slab_spec(arr, layer, n_steps, nt, stacked):
    rows, cols = arr.shape[-2:]
    n_slabs = 1
    while n_slabs * 2 <= n_steps and rows % (n_slabs * 2 * DIAG_TILE) == 0:
        n_slabs *= 2
    per = n_steps // n_slabs
    slab = lambda i, j: jnp.minimum((i * nt + j) // per, n_slabs - 1)
    if stacked:
        return pl.BlockSpec((None, rows // n_slabs, cols), lambda i, j: (layer, slab(i, j), 0))
    return pl.BlockSpec((rows // n_slabs, cols), lambda i, j: (slab(i, j), 0))


def _mixer(x, cos_t, sin_t, layer_params, emat, ffn_weights, layer, tb):
    b, t, d = x.shape
    nt = t // tb
    n_steps = b * nt
    tok = lambda width: pl.BlockSpec((1, tb, width), lambda i, j: (i, j, 0))
    next_tok = pl.BlockSpec((1, tb, d), lambda i, j: (jnp.minimum(i + (j + 1) // nt, b - 1), (j + 1) % nt, 0))
    const_specs = [_layer_spec(p, layer, 2) for p in layer_params]
    const_specs.append(pl.BlockSpec(emat.shape, lambda i, j: (0, 0), pipeline_mode=pl.Buffered(1)))
    scratch = [
        pltpu.VMEM((tb, D_MODEL), BF16),
        pltpu.VMEM((tb, D_MODEL), BF16),
        pltpu.VMEM((tb, 3 * CONV_DIM), F32),
        pltpu.VMEM((tb, 4 * RET_DIM), F32),
        pltpu.VMEM((tb, GLA_COLS), F32),
        pltpu.VMEM((tb, N_BRANCHES * D_MODEL), F32),
        pltpu.VMEM((tb, RET_DIM), F32),
        pltpu.VMEM((tb, GLA_VAL_DIM), F32),
        pltpu.VMEM((tb, D_MODEL), F32),
        pltpu.VMEM((tb + SUBLANES, CONV_DIM), F32),
        pltpu.VMEM((RET_HEADS, RET_HEAD_DIM, RET_HEAD_DIM), F32),
        pltpu.VMEM((GLA_HEADS // 2, 2 * GLA_VAL_HEAD_DIM, LANES), F32),
        pltpu.VMEM((tb, GLA_KEY_DIM), F32),
        pltpu.VMEM((tb, GLA_SUB * GLA_KEY_DIM), BF16),
        pltpu.VMEM((tb, GLA_KEY_DIM), F32),
        pltpu.VMEM((RET_HEADS, tb, tb), F32),
        pltpu.VMEM((2 * RET_HEADS, tb, RET_HEAD_DIM), F32),
    ]
    return pl.pallas_call(
        functools.partial(_mixer_kernel, tb=tb),
        grid=(b, t // tb),
        in_specs=[tok(d), next_tok, tok(RET_HEAD_DIM), tok(RET_HEAD_DIM)] + const_specs
        + [_slab_spec(w, layer, n_steps, nt, True) for w in ffn_weights],
        out_specs=[tok(d)] + [_slab_spec(w, layer, n_steps, nt, False) for w in ffn_weights],
        out_shape=[jax.ShapeDtypeStruct(x.shape, x.dtype)]
        + [jax.ShapeDtypeStruct(w.shape[1:], BF16) for w in ffn_weights],
        scratch_shapes=scratch,
        compiler_params=pltpu.CompilerParams(dimension_semantics=("arbitrary", "arbitrary"),
                                             vmem_limit_bytes=VMEM_LIMIT_BYTES),
        name="mixer",
    )(x, x, cos_t, sin_t, *layer_params, emat, *ffn_weights)


def _ffn_kernel(x_ref, npre_ref, wg_ref, wu_ref, wd_ref, npost_ref, out_ref, *, fc):
    x = x_ref[...]
    h = _rms(x, npre_ref[...]).astype(BF16)
    acc = jnp.zeros(x.shape, F32)
    for f in range(0, D_FF, fc):
        g = _dot(h, wg_ref[:, f:f + fc])
        u = _dot(h, wu_ref[:, f:f + fc])
        acc = acc + _dot((_silu(g) * u).astype(BF16), wd_ref[f:f + fc, :])
    out_ref[...] = x + _rms(acc, npost_ref[...])


def _ffn(x2, npre, weights, npost, layer, tm, fc):
    n, d = x2.shape
    resident = lambda w: pl.BlockSpec(w.shape, lambda i: (0, 0), pipeline_mode=pl.Buffered(1))
    return pl.pallas_call(
        functools.partial(_ffn_kernel, fc=fc),
        grid=(n // tm,),
        in_specs=[pl.BlockSpec((tm, d), lambda i: (i, 0)), _layer_spec(npre, layer, 1)]
        + [resident(w) for w in weights] + [_layer_spec(npost, layer, 1)],
        out_specs=pl.BlockSpec((tm, d), lambda i: (i, 0)),
        out_shape=jax.ShapeDtypeStruct(x2.shape, x2.dtype),
        compiler_params=pltpu.CompilerParams(dimension_semantics=("arbitrary",),
                                             vmem_limit_bytes=VMEM_LIMIT_BYTES),
        name="swiglu",
    )(x2, npre, *weights, npost)


def _pick_block(t, want):
    tb = min(want, t)
    while t % tb:
        tb //= 2
    return tb


def kernel(x, positions, norm_mix_pre, w_in, conv_w, ret_gn_w, gla_w_a2, gla_b_a, gla_gn_w, w_branch_a, w_branch_b, w_branch_c, w_out, norm_mix_post, norm_ffn_pre, w_ffn_gate, w_ffn_up, w_ffn_down, norm_ffn_post):
    b, t, d = x.shape
    depth = w_in.shape[0]
    tb = _pick_block(t, MIXER_BLOCK)
    tm = _pick_block(b * t, FFN_BLOCK)
    cos_t, sin_t = _rope_tables(positions, _pick_block(t, ROPE_BLOCK))
    emat = _diag_reduce_matrix()
    rows = lambda v: v.reshape(depth, 1, -1)
    pad_rank = LANES - GLA_GATE_RANK
    w_packed = jnp.concatenate(
        [w_in[:, :, :OFF_GA], w_in[:, :, OFF_GATES:],
         jnp.pad(w_in[:, :, OFF_GA:OFF_GATES], ((0, 0), (0, 0), (0, pad_rank)))], axis=2).astype(BF16)
    wa2 = jnp.pad(gla_w_a2, ((0, 0), (0, pad_rank), (0, 0))).astype(BF16)
    mixer_params = [rows(norm_mix_pre), w_packed, conv_w, rows(ret_gn_w), wa2, rows(gla_b_a), rows(gla_gn_w),
                    w_branch_a.astype(BF16), w_branch_b.astype(BF16), w_branch_c.astype(BF16),
                    w_out.astype(BF16), rows(norm_mix_post)]
    ffn_weights = [w_ffn_gate, w_ffn_up, w_ffn_down]
    for l in range(depth):
        x, *ffn_bf16 = _mixer(x, cos_t, sin_t, mixer_params, emat, ffn_weights, l, tb)
        x = _ffn(x.reshape(b * t, d), rows(norm_ffn_pre), ffn_bf16, rows(norm_ffn_post), l, tm,
                 MXU_COLS).reshape(b, t, d)
    return x
```

```python
import functools
import math

import jax
import jax.numpy as jnp
import numpy as np
from jax import lax
from jax.experimental import pallas as pl
from jax.experimental.pallas import tpu as pltpu

D_MODEL = 1024
CONV_DIM = 512
CONV_WIDTH = 3
RET_HEADS = 4
RET_HEAD_DIM = 128
RET_DIM = RET_HEADS * RET_HEAD_DIM
GLA_HEADS = 4
GLA_KEY_HEAD_DIM = 64
GLA_VAL_HEAD_DIM = 128
GLA_KEY_DIM = GLA_HEADS * GLA_KEY_HEAD_DIM
GLA_VAL_DIM = GLA_HEADS * GLA_VAL_HEAD_DIM
GLA_GATE_RANK = 16
GLA_GATE_TAU = 16.0
N_BRANCHES = 3
D_FF = 2816
ROPE_BASE = 10000.0
EPS = 1e-6
LOG2E = 1.0 / math.log(2.0)

OFF_CONV = 0
OFF_RET = 3 * CONV_DIM
OFF_GLA = OFF_RET + 4 * RET_DIM
OFF_GA = OFF_GLA + 2 * GLA_KEY_DIM + 2 * GLA_VAL_DIM
OFF_GATES = OFF_GA + GLA_GATE_RANK
GLA_COLS = 2 * GLA_KEY_DIM + 2 * GLA_VAL_DIM

LANES = 128
W_GATES = OFF_GA
W_GA = W_GATES + N_BRANCHES * D_MODEL
W_COLS = W_GA + LANES
SUBLANES = 8
MXU_COLS = 256
GLA_CHUNK = 64
GLA_SUB = 8
DIAG_TILE = 16
VMEM_LIMIT_BYTES = 56 * 1024 * 1024
MIXER_BLOCK = 256
FFN_BLOCK = 1024
ROPE_BLOCK = 2048

BF16 = jnp.bfloat16
F32 = jnp.float32


def _dot(a, b):
    return jnp.dot(a, b, preferred_element_type=F32)


def _dot_nt(a, b):
    return lax.dot_general(a, b, (((1,), (1,)), ((), ())), preferred_element_type=F32)


def _dot_tn(a, b):
    return lax.dot_general(a, b, (((0,), (0,)), ((), ())), preferred_element_type=F32)


def _rms(x, w):
    return x * lax.rsqrt(jnp.mean(x * x, axis=-1, keepdims=True) + EPS) * w


def _sigmoid(x):
    return 1.0 / (1.0 + jnp.exp2(x * -LOG2E))


def _silu(x):
    return x * _sigmoid(x)


def _log_sigmoid(x):
    return jnp.minimum(x, 0.0) - jnp.log(1.0 + jnp.exp2(jnp.abs(x) * -LOG2E))


def _rope_kernel(pos_ref, invf_ref, cos_ref, sin_ref):
    tb = pos_ref.shape[1]
    hb = tb // 2
    half = RET_HEAD_DIM // 2
    pos = pos_ref[0].astype(F32)
    lo = lax.broadcasted_iota(jnp.int32, (hb, RET_HEAD_DIM), 1) < half
    ang = jnp.where(lo, pos[0:hb], pos[hb:tb]) * invf_ref[...]
    c = jnp.cos(ang)
    s = jnp.sin(ang)
    c_sw = pltpu.roll(c, half, axis=1)
    s_sw = pltpu.roll(s, half, axis=1)
    sign = jnp.where(lo, -1.0, 1.0)
    cos_ref[0, 0:hb, :] = jnp.where(lo, c, c_sw)
    cos_ref[0, hb:tb, :] = jnp.where(lo, c_sw, c)
    sin_ref[0, 0:hb, :] = jnp.where(lo, s, s_sw) * sign
    sin_ref[0, hb:tb, :] = jnp.where(lo, s_sw, s) * sign


def _rope_tables(positions, tb):
    b, t = positions.shape
    half = RET_HEAD_DIM // 2
    inv_freq = ROPE_BASE ** (-jnp.arange(half, dtype=F32) / half)
    invf = jnp.concatenate([inv_freq, inv_freq]).reshape(1, RET_HEAD_DIM)
    pos3 = positions.reshape(b, t, 1)
    out = jax.ShapeDtypeStruct((b, t, RET_HEAD_DIM), F32)
    return pl.pallas_call(
        _rope_kernel,
        grid=(b, t // tb),
        in_specs=[pl.BlockSpec((1, tb, 1), lambda i, j: (i, j, 0)),
                  pl.BlockSpec((1, RET_HEAD_DIM), lambda i, j: (0, 0))],
        out_specs=[pl.BlockSpec((1, tb, RET_HEAD_DIM), lambda i, j: (i, j, 0)),
                   pl.BlockSpec((1, tb, RET_HEAD_DIM), lambda i, j: (i, j, 0))],
        out_shape=[out, out],
        name="rope_tables",
    )(pos3, invf)


def _mixer_kernel(x_ref, xnext_ref, cos_ref, sin_ref, npre_ref, win_ref, convw_ref,
                  retgn_ref, wa2_ref, ba_ref, glagn_ref, wba_ref, wbb_ref, wbc_ref, wout_ref,
                  npost_ref, emat_ref, wg32_ref, wu32_ref, wd32_ref, out_ref, wg16_ref, wu16_ref, wd16_ref,
                  h_s, hn_s, pa_s, pb_s, pc_s, gate_s, ob_s, oc_s, m_s, zc_s, rstate_s, gstate_s, cum_s,
                  td_s, srep_s, dmask_s, rdec_s, *, tb):
    b_idx = pl.program_id(0)
    t_idx = pl.program_id(1)

    @pl.when((b_idx == 0) & (t_idx == 0))
    def _():
        ri = lax.broadcasted_iota(jnp.int32, (tb, tb), 0)
        ci = lax.broadcasted_iota(jnp.int32, (tb, tb), 1)
        dif = jnp.maximum(ri - ci, 0).astype(F32)
        rowf = lax.broadcasted_iota(jnp.int32, (tb, RET_HEAD_DIM), 0).astype(F32)
        for hh in range(RET_HEADS):
            log_g = math.log(1.0 - 2.0 ** (-5.0 - hh))
            dmask_s[hh] = jnp.where(ri >= ci, jnp.exp(log_g * dif), 0.0)
            rdec_s[2 * hh] = jnp.exp(log_g * (rowf + 1.0))
            rdec_s[2 * hh + 1] = jnp.exp(log_g * (tb - 1.0 - rowf))

    @pl.when(t_idx == 0)
    def _():
        zc_s[0:SUBLANES, :] = jnp.zeros((SUBLANES, CONV_DIM), F32)
        rstate_s[...] = jnp.zeros(rstate_s.shape, F32)
        gstate_s[...] = jnp.zeros(gstate_s.shape, F32)

    def proj_task(dst_ref, w_ref, w_lo, lo, act=None, h_ref=h_s):
        def run():
            r = _dot(h_ref[...], w_ref[:, w_lo + lo:w_lo + lo + MXU_COLS])
            dst_ref[:, lo:lo + MXU_COLS] = r if act is None else act(r)
        return run

    ret_q_cols = range(0, RET_DIM, MXU_COLS)

    @pl.when((b_idx == 0) & (t_idx == 0))
    def _():
        h_s[...] = _rms(x_ref[0], npre_ref[...]).astype(BF16)
        for lo in ret_q_cols:
            proj_task(pb_s, win_ref, OFF_RET, lo)()

    half_tb = tb // 2

    def next_h_task(r0):
        def run():
            xn = xnext_ref[0, r0:r0 + half_tb, :]
            hn_s[r0:r0 + half_tb, :] = _rms(xn, npre_ref[...]).astype(BF16)
        return run

    def ffn_cast_task():
        wg16_ref[...] = wg32_ref[...].astype(BF16)
        wu16_ref[...] = wu32_ref[...].astype(BF16)
        wd16_ref[...] = wd32_ref[...].astype(BF16)

    def interleave(primary, fillers):
        done = 0
        for i, step in enumerate(primary):
            step()
            while done * len(primary) < (i + 1) * len(fillers):
                fillers[done]()
                done += 1

    ret_proj = [proj_task(pb_s, win_ref, OFF_RET, lo) for lo in range(RET_DIM, 4 * RET_DIM, MXU_COLS)]
    conv_proj = [proj_task(pa_s, win_ref, OFF_CONV, lo) for lo in range(0, 3 * CONV_DIM, MXU_COLS)]
    gate_proj = [proj_task(gate_s, win_ref, W_GATES, lo, _sigmoid) for lo in range(0, N_BRANCHES * D_MODEL, MXU_COLS)]

    gc = GLA_CHUNK
    kd = GLA_KEY_DIM
    qscale = GLA_KEY_HEAD_DIM ** -0.5
    n_chunks = tb // gc
    gla_proj = [proj_task(pc_s, win_ref, OFF_GLA, lo) for lo in range(0, GLA_COLS, MXU_COLS)]
    ga_down = _dot(h_s[...], win_ref[:, W_GA:W_COLS])
    logits = _dot(ga_down.astype(BF16), wa2_ref[...]) + ba_ref[...]
    for task in gla_proj[:3]:
        task()
    log_a = _log_sigmoid(logits) * (LOG2E / GLA_GATE_TAU)
    ri = lax.broadcasted_iota(jnp.int32, (gc, gc), 0)
    ci = lax.broadcasted_iota(jnp.int32, (gc, gc), 1)
    tril = jnp.where(ri >= ci, 1.0, 0.0).astype(BF16)
    for c in range(n_chunks):
        g = log_a[c * gc:(c + 1) * gc, :]
        g_hi = g.astype(BF16)
        g_lo = (g - g_hi.astype(F32)).astype(BF16)
        cum_s[c * gc:(c + 1) * gc, :] = _dot(tril, g_hi) + _dot(tril, g_lo)
    for task in gla_proj[3:]:
        task()

    def diag_unit(u):
        r0 = u * DIAG_TILE
        subs = []
        for rs in range(r0, r0 + DIAG_TILE, GLA_SUB):
            subs.append((rs, pc_s[rs:rs + GLA_SUB, 0:kd] * qscale, cum_s[rs:rs + GLA_SUB, :]))
        for j in range(GLA_SUB):
            tjs = []
            for rs, qb, cb in subs:
                kj = pc_s[rs + j:rs + j + 1, kd:2 * kd]
                cj = cum_s[rs + j:rs + j + 1, :]
                tjs.append(qb * kj * jnp.exp2(jnp.minimum(cb - cj, 0.0)))
            td_s[r0:r0 + DIAG_TILE, j * kd:(j + 1) * kd] = jnp.concatenate(tjs, axis=0).astype(BF16)

    interleave([functools.partial(diag_unit, u) for u in range(tb // DIAG_TILE)], ret_proj + conv_proj[:4])

    def conv_task():
        z = pa_s[:, 2 * CONV_DIM:3 * CONV_DIM] * pa_s[:, 0:CONV_DIM]
        zc_s[SUBLANES:SUBLANES + tb, :] = z
        z1 = zc_s[SUBLANES - 1:SUBLANES - 1 + tb, :]
        z2 = zc_s[SUBLANES - 2:SUBLANES - 2 + tb, :]
        cw = convw_ref[...]
        y_a = pa_s[:, CONV_DIM:2 * CONV_DIM] * (cw[0:1, :] * z2 + cw[1:2, :] * z1 + cw[2:3, :] * z)
        zc_s[0:SUBLANES, :] = z[tb - SUBLANES:tb, :]
        m_s[...] = gate_s[:, 0:D_MODEL] * _dot(y_a.astype(BF16), wba_ref[...])

    ret = [dict() for _ in range(RET_HEADS)]

    def ret_rotary(hh):
        lo = hh * RET_HEAD_DIM
        cosf = cos_ref[0]
        sins = sin_ref[0]
        qh = pb_s[:, lo:lo + RET_HEAD_DIM]
        kh = pb_s[:, RET_DIM + lo:RET_DIM + lo + RET_HEAD_DIM]
        qc = (qh * cosf + pltpu.roll(qh, RET_HEAD_DIM // 2, axis=1) * sins) * (RET_HEAD_DIM ** -0.5)
        kc = kh * cosf + pltpu.roll(kh, RET_HEAD_DIM // 2, axis=1) * sins
        ret[hh].update(q=qc.astype(BF16), k=kc.astype(BF16),
                       qd=(qc * rdec_s[2 * hh]).astype(BF16), kd=(kc * rdec_s[2 * hh + 1]).astype(BF16),
                       v=pb_s[:, 2 * RET_DIM + lo:2 * RET_DIM + lo + RET_HEAD_DIM].astype(BF16))

    def ret_scores(hh):
        r = ret[hh]
        log_g = math.log(1.0 - 2.0 ** (-5.0 - hh))
        state = rstate_s[hh]
        r["s"] = (_dot_nt(r["q"], r["k"]) * dmask_s[hh]).astype(BF16)
        r["inter"] = _dot(r["qd"], state.astype(BF16))
        rstate_s[hh] = state * math.exp(log_g * tb) + _dot_tn(r["kd"], r["v"])

    def ret_output(hh):
        r = ret[hh]
        lo = hh * RET_HEAD_DIM
        o = _dot(r["s"], r["v"]) + r["inter"]
        oc = o - jnp.mean(o, axis=-1, keepdims=True)
        ob_s[:, lo:lo + RET_HEAD_DIM] = oc * lax.rsqrt(jnp.mean(oc * oc, axis=-1, keepdims=True) + EPS)

    rowi = lax.broadcasted_iota(jnp.int32, (gc, LANES), 0)
    lanei = lax.broadcasted_iota(jnp.int32, (gc, LANES), 1)
    colj = lanei & (GLA_KEY_HEAD_DIM - 1)
    sub_shift = GLA_SUB.bit_length() - 1
    diag_mask = ((colj >> sub_shift) == (rowi >> sub_shift)) & (colj <= rowi)
    head_lo = lanei < GLA_KEY_HEAD_DIM
    srow = lax.broadcasted_iota(jnp.int32, (2 * GLA_VAL_HEAD_DIM, LANES), 0)
    slane = lax.broadcasted_iota(jnp.int32, (2 * GLA_VAL_HEAD_DIM, LANES), 1)
    state_mask = (srow < GLA_VAL_HEAD_DIM) == (slane < GLA_KEY_HEAD_DIM)
    zero_v = jnp.zeros((gc, GLA_VAL_HEAD_DIM), BF16)
    block_masks = {}
    half = gc // 4
    while half >= GLA_SUB:
        shift = (2 * half).bit_length() - 1
        block_masks[half] = (colj >> shift) == (rowi >> shift)
        half //= 2

    gla = [[dict() for _ in range(GLA_HEADS // 2)] for _ in range(n_chunks)]

    def gla_intra(c):
        rows = slice(c * gc, (c + 1) * gc)
        cum = cum_s[rows, :]
        last = cum[gc - 1:gc, :]
        q = pc_s[rows, 0:kd] * qscale
        k = pc_s[rows, kd:2 * kd]
        q_in = q * jnp.exp2(cum)
        k_st = k * jnp.exp2(last - cum)
        levels = []
        half = gc // 2
        while half >= GLA_SUB:
            a_rows, b_rows = [], []
            zero_half = jnp.zeros((half, kd), F32)
            for base in range(0, gc, 2 * half):
                mid = base + half
                ref = cum[mid - 1:mid, :]
                b_rows += [k[base:mid, :] * jnp.exp2(ref - cum[base:mid, :]), zero_half]
                a_rows += [zero_half, q[mid:mid + half, :] * jnp.exp2(cum[mid:mid + half, :] - ref)]
            levels.append((half, jnp.concatenate(a_rows, axis=0), jnp.concatenate(b_rows, axis=0)))
            half //= 2
        for pr in range(GLA_HEADS // 2):
            ls = slice(pr * LANES, (pr + 1) * LANES)
            s_off = None
            for half, a, b in levels:
                b_two = jnp.concatenate([jnp.where(head_lo, b[:, ls], 0.0).astype(BF16),
                                         jnp.where(head_lo, 0.0, b[:, ls]).astype(BF16)], axis=0)
                r = _dot_nt(a[:, ls].astype(BF16), b_two)
                if 2 * half < gc:
                    r = jnp.where(block_masks[half], r, 0.0)
                s_off = r if s_off is None else s_off + r
            s = jnp.where(diag_mask, srep_s[rows, ls], s_off).astype(BF16)
            v_lo = 2 * kd + 2 * pr * GLA_VAL_HEAD_DIM
            v0 = pc_s[rows, v_lo:v_lo + GLA_VAL_HEAD_DIM].astype(BF16)
            v1 = pc_s[rows, v_lo + GLA_VAL_HEAD_DIM:v_lo + 2 * GLA_VAL_HEAD_DIM].astype(BF16)
            v_bd = jnp.concatenate([jnp.concatenate([v0, zero_v], axis=1),
                                    jnp.concatenate([zero_v, v1], axis=1)], axis=0)
            v_pair = jnp.concatenate([v0, v1], axis=1)
            upd = _dot_tn(v_pair, k_st[:, ls].astype(BF16))
            gla[c][pr].update(intra=_dot(s, v_bd), q_in=q_in[:, ls].astype(BF16),
                              upd=jnp.where(state_mask, upd, 0.0), decay=jnp.exp2(last[:, ls]))

    def gla_carry():
        for pr in range(GLA_HEADS // 2):
            st = gstate_s[pr]
            for c in range(n_chunks):
                g = gla[c][pr]
                o = g["intra"] + _dot_nt(g["q_in"], st.astype(BF16))
                st = st * g["decay"] + g["upd"]
                for hl in range(2):
                    oh = o[:, hl * GLA_VAL_HEAD_DIM:(hl + 1) * GLA_VAL_HEAD_DIM]
                    lo = (2 * pr + hl) * GLA_VAL_HEAD_DIM
                    oc_s[c * gc:(c + 1) * gc, lo:lo + GLA_VAL_HEAD_DIM] = (
                        oh * lax.rsqrt(jnp.mean(oh * oh, axis=-1, keepdims=True) + EPS))
            gstate_s[pr] = st

    def srep_task():
        srep_s[...] = _dot(td_s[...], emat_ref[...])

    heads = range(RET_HEADS)
    interleave([next_h_task(0), next_h_task(half_tb), ffn_cast_task]
               + [functools.partial(ret_rotary, hh) for hh in heads], conv_proj[4:])
    interleave([srep_task] + [functools.partial(ret_scores, hh) for hh in heads], gate_proj[:4])
    interleave([functools.partial(gla_intra, c) for c in range(n_chunks)], gate_proj[4:8])
    interleave([functools.partial(ret_output, hh) for hh in heads], gate_proj[8:])
    h_s[...] = hn_s[...]
    interleave([gla_carry], [conv_task])

    y_b = ob_s[...] * retgn_ref[...] * _silu(pb_s[:, 3 * RET_DIM:4 * RET_DIM])
    m_s[...] += gate_s[:, D_MODEL:2 * D_MODEL] * _dot(y_b.astype(BF16), wbb_ref[...])
    y_c = oc_s[...] * glagn_ref[...] * _silu(pc_s[:, 2 * kd + GLA_VAL_DIM:2 * kd + 2 * GLA_VAL_DIM])
    merged = m_s[...] + gate_s[:, 2 * D_MODEL:3 * D_MODEL] * _dot(y_c.astype(BF16), wbc_ref[...])

    merged_b = merged.astype(BF16)
    sumsq = jnp.zeros((tb, 1), F32)
    for lo in range(0, D_MODEL, MXU_COLS):
        blk = _dot(merged_b, wout_ref[:, lo:lo + MXU_COLS])
        m_s[:, lo:lo + MXU_COLS] = blk
        sumsq = sumsq + jnp.sum(blk * blk, axis=-1, keepdims=True)
    for lo in ret_q_cols:
        proj_task(pb_s, win_ref, OFF_RET, lo, h_ref=hn_s)()
    scale = lax.rsqrt(sumsq * (1.0 / D_MODEL) + EPS)
    out_ref[0] = x_ref[0] + m_s[...] * scale * npost_ref[...]


def _layer_spec(arr, layer, grid_rank):
    index = (lambda i: (layer, 0, 0)) if grid_rank == 1 else (lambda i, j: (layer, 0, 0))
    return pl.BlockSpec((None,) + arr.shape[1:], index, pipeline_mode=pl.Buffered(1))


def _diag_reduce_matrix():
    kidx = np.arange(GLA_SUB * GLA_KEY_DIM)
    j = kidx // GLA_KEY_DIM
    hk = (kidx % GLA_KEY_DIM) // GLA_KEY_HEAD_DIM
    col = np.arange(GLA_KEY_DIM)
    hc = col // GLA_KEY_HEAD_DIM
    cj = (col % GLA_KEY_HEAD_DIM) % GLA_SUB
    return jnp.asarray((hk[:, None] == hc[None, :]) & (j[:, None] == cj[None, :]), dtype=BF16)


def _slab_spec(arr, layer, n_steps, nt, stacked):
    rows, cols = arr.shape[-2:]
    n_slabs = 1
    while n_slabs * 2 <= n_steps and rows % (n_slabs * 2 * DIAG_TILE) == 0:
        n_slabs *= 2
    per = n_steps // n_slabs
    slab = lambda i, j: jnp.minimum((i * nt + j) // per, n_slabs - 1)
    if stacked:
        return pl.BlockSpec((None, rows // n_slabs, cols), lambda i, j: (layer, slab(i, j), 0))
    return pl.BlockSpec((rows // n_slabs, cols), lambda i, j: (slab(i, j), 0))


def _mixer(x, cos_t, sin_t, layer_params, emat, ffn_weights, layer, tb):
    b, t, d = x.shape
    nt = t // tb
    n_steps = b * nt
    tok = lambda width: pl.BlockSpec((1, tb, width), lambda i, j: (i, j, 0))
    next_tok = pl.BlockSpec((1, tb, d), lambda i, j: (jnp.minimum(i + (j + 1) // nt, b - 1), (j + 1) % nt, 0))
    const_specs = [_layer_spec(p, layer, 2) for p in layer_params]
    const_specs.append(pl.BlockSpec(emat.shape, lambda i, j: (0, 0), pipeline_mode=pl.Buffered(1)))
    scratch = [
        pltpu.VMEM((tb, D_MODEL), BF16),
        pltpu.VMEM((tb, D_MODEL), BF16),
        pltpu.VMEM((tb, 3 * CONV_DIM), F32),
        pltpu.VMEM((tb, 4 * RET_DIM), F32),
        pltpu.VMEM((tb, GLA_COLS), F32),
        pltpu.VMEM((tb, N_BRANCHES * D_MODEL), F32),
        pltpu.VMEM((tb, RET_DIM), F32),
        pltpu.VMEM((tb, GLA_VAL_DIM), F32),
        pltpu.VMEM((tb, D_MODEL), F32),
        pltpu.VMEM((tb + SUBLANES, CONV_DIM), F32),
        pltpu.VMEM((RET_HEADS, RET_HEAD_DIM, RET_HEAD_DIM), F32),
        pltpu.VMEM((GLA_HEADS // 2, 2 * GLA_VAL_HEAD_DIM, LANES), F32),
        pltpu.VMEM((tb, GLA_KEY_DIM), F32),
        pltpu.VMEM((tb, GLA_SUB * GLA_KEY_DIM), BF16),
        pltpu.VMEM((tb, GLA_KEY_DIM), F32),
        pltpu.VMEM((RET_HEADS, tb, tb), F32),
        pltpu.VMEM((2 * RET_HEADS, tb, RET_HEAD_DIM), F32),
    ]
    return pl.pallas_call(
        functools.partial(_mixer_kernel, tb=tb),
        grid=(b, t // tb),
        in_specs=[tok(d), next_tok, tok(RET_HEAD_DIM), tok(RET_HEAD_DIM)] + const_specs
        + [_slab_spec(w, layer, n_steps, nt, True) for w in ffn_weights],
        out_specs=[tok(d)] + [_slab_spec(w, layer, n_steps, nt, False) for w in ffn_weights],
        out_shape=[jax.ShapeDtypeStruct(x.shape, x.dtype)]
        + [jax.ShapeDtypeStruct(w.shape[1:], BF16) for w in ffn_weights],
        scratch_shapes=scratch,
        compiler_params=pltpu.CompilerParams(dimension_semantics=("arbitrary", "arbitrary"),
                                             vmem_limit_bytes=VMEM_LIMIT_BYTES),
        name="mixer",
    )(x, x, cos_t, sin_t, *layer_params, emat, *ffn_weights)


def _ffn_kernel(x_ref, npre_ref, wg_ref, wu_ref, wd_ref, npost_ref, out_ref, *, fc):
    tm = x_ref.shape[0]
    hm = tm // 2
    halves = (slice(0, hm), slice(hm, tm))
    h_halves = [_rms(x_ref[rows, :], npre_ref[...]).astype(BF16) for rows in halves]
    chunks = list(range(0, D_FF, fc))
    assert len(chunks) >= 2 and D_FF % fc == 0

    def hidden(h, f):
        g = _dot(h, wg_ref[:, f:f + fc])
        u = _dot(h, wu_ref[:, f:f + fc])
        return (_silu(g) * u).astype(BF16)

    f0 = chunks[0]
    acc = jnp.concatenate([_dot(hidden(h, f0), wd_ref[f0:f0 + fc, :]) for h in h_halves], axis=0)
    h = jnp.concatenate(h_halves, axis=0)
    for f in chunks[1:-1]:
        acc = acc + _dot(hidden(h, f), wd_ref[f:f + fc, :])
    f1 = chunks[-1]
    for rows, h_half in zip(halves, h_halves):
        total = acc[rows, :] + _dot(hidden(h_half, f1), wd_ref[f1:f1 + fc, :])
        out_ref[rows, :] = x_ref[rows, :] + _rms(total, npost_ref[...])


def _ffn(x2, npre, weights, npost, layer, tm, fc):
    n, d = x2.shape
    resident = lambda w: pl.BlockSpec(w.shape, lambda i: (0, 0), pipeline_mode=pl.Buffered(1))
    return pl.pallas_call(
        functools.partial(_ffn_kernel, fc=fc),
        grid=(n // tm,),
        in_specs=[pl.BlockSpec((tm, d), lambda i: (i, 0)), _layer_spec(npre, layer, 1)]
        + [resident(w) for w in weights] + [_layer_spec(npost, layer, 1)],
        out_specs=pl.BlockSpec((tm, d), lambda i: (i, 0)),
        out_shape=jax.ShapeDtypeStruct(x2.shape, x2.dtype),
        compiler_params=pltpu.CompilerParams(dimension_semantics=("arbitrary",),
                                             vmem_limit_bytes=VMEM_LIMIT_BYTES),
        name="swiglu",
    )(x2, npre, *weights, npost)


def _pick_block(t, want):
    tb = min(want, t)
    while t % tb:
        tb //= 2
    return tb


def kernel(x, positions, norm_mix_pre, w_in, conv_w, ret_gn_w, gla_w_a2, gla_b_a, gla_gn_w, w_branch_a, w_branch_b, w_branch_c, w_out, norm_mix_post, norm_ffn_pre, w_ffn_gate, w_ffn_up, w_ffn_down, norm_ffn_post):
    b, t, d = x.shape
    depth = w_in.shape[0]
    tb = _pick_block(t, MIXER_BLOCK)
    tm = _pick_block(b * t, FFN_BLOCK)
    cos_t, sin_t = _rope_tables(positions, _pick_block(t, ROPE_BLOCK))
    emat = _diag_reduce_matrix()
    rows = lambda v: v.reshape(depth, 1, -1)
    pad_rank = LANES - GLA_GATE_RANK
    w_packed = jnp.concatenate(
        [w_in[:, :, :OFF_GA], w_in[:, :, OFF_GATES:],
         jnp.pad(w_in[:, :, OFF_GA:OFF_GATES], ((0, 0), (0, 0), (0, pad_rank)))], axis=2).astype(BF16)
    wa2 = jnp.pad(gla_w_a2, ((0, 0), (0, pad_rank), (0, 0))).astype(BF16)
    mixer_params = [rows(norm_mix_pre), w_packed, conv_w, rows(ret_gn_w), wa2, rows(gla_b_a), rows(gla_gn_w),
                    w_branch_a.astype(BF16), w_branch_b.astype(BF16), w_branch_c.astype(BF16),
                    w_out.astype(BF16), rows(norm_mix_post)]
    ffn_weights = [w_ffn_gate, w_ffn_up, w_ffn_down]
    for l in range(depth):
        x, *ffn_bf16 = _mixer(x, cos_t, sin_t, mixer_params, emat, ffn_weights, l, tb)
        x = _ffn(x.reshape(b * t, d), rows(norm_ffn_pre), ffn_bf16, rows(norm_ffn_post), l, tm,
                 MXU_COLS).reshape(b, t, d)
    return x
```

```python
import functools
import math

import jax
import jax.numpy as jnp
import numpy as np
from jax import lax
from jax.experimental import pallas as pl
from jax.experimental.pallas import tpu as pltpu

D_MODEL = 1024
CONV_DIM = 512
CONV_WIDTH = 3
RET_HEADS = 4
RET_HEAD_DIM = 128
RET_DIM = RET_HEADS * RET_HEAD_DIM
GLA_HEADS = 4
GLA_KEY_HEAD_DIM = 64
GLA_VAL_HEAD_DIM = 128
GLA_KEY_DIM = GLA_HEADS * GLA_KEY_HEAD_DIM
GLA_VAL_DIM = GLA_HEADS * GLA_VAL_HEAD_DIM
GLA_GATE_RANK = 16
GLA_GATE_TAU = 16.0
N_BRANCHES = 3
D_FF = 2816
ROPE_BASE = 10000.0
EPS = 1e-6
LOG2E = 1.0 / math.log(2.0)

OFF_CONV = 0
OFF_RET = 3 * CONV_DIM
OFF_GLA = OFF_RET + 4 * RET_DIM
OFF_GA = OFF_GLA + 2 * GLA_KEY_DIM + 2 * GLA_VAL_DIM
OFF_GATES = OFF_GA + GLA_GATE_RANK
GLA_COLS = 2 * GLA_KEY_DIM + 2 * GLA_VAL_DIM

LANES = 128
W_GATES = OFF_GA
W_GA = W_GATES + N_BRANCHES * D_MODEL
SUBLANES = 8
MXU_COLS = 256
W_COLS = W_GA + MXU_COLS
GLA_CHUNK = 64
GLA_SUB = 8
DIAG_TILE = 16
VMEM_LIMIT_BYTES = 56 * 1024 * 1024
MIXER_BLOCK = 256
FFN_BLOCK = 1024
ROPE_BLOCK = 2048

BF16 = jnp.bfloat16
F32 = jnp.float32


def _dot(a, b):
    return jnp.dot(a, b, preferred_element_type=F32)


def _dot_nt(a, b):
    return lax.dot_general(a, b, (((1,), (1,)), ((), ())), preferred_element_type=F32)


def _dot_tn(a, b):
    return lax.dot_general(a, b, (((0,), (0,)), ((), ())), preferred_element_type=F32)


def _rms(x, w):
    return x * lax.rsqrt(jnp.mean(x * x, axis=-1, keepdims=True) + EPS) * w


def _sigmoid(x):
    return 1.0 / (1.0 + jnp.exp2(x * -LOG2E))


def _silu(x):
    return x * _sigmoid(x)


def _log_sigmoid(x):
    return jnp.minimum(x, 0.0) - jnp.log(1.0 + jnp.exp2(jnp.abs(x) * -LOG2E))


def _rope_kernel(pos_ref, invf_ref, cos_ref, sin_ref):
    tb = pos_ref.shape[1]
    hb = tb // 2
    half = RET_HEAD_DIM // 2
    pos = pos_ref[0].astype(F32)
    lo = lax.broadcasted_iota(jnp.int32, (hb, RET_HEAD_DIM), 1) < half
    ang = jnp.where(lo, pos[0:hb], pos[hb:tb]) * invf_ref[...]
    c = jnp.cos(ang)
    s = jnp.sin(ang)
    c_sw = pltpu.roll(c, half, axis=1)
    s_sw = pltpu.roll(s, half, axis=1)
    sign = jnp.where(lo, -1.0, 1.0)
    cos_ref[0, 0:hb, :] = jnp.where(lo, c, c_sw)
    cos_ref[0, hb:tb, :] = jnp.where(lo, c_sw, c)
    sin_ref[0, 0:hb, :] = jnp.where(lo, s, s_sw) * sign
    sin_ref[0, hb:tb, :] = jnp.where(lo, s_sw, s) * sign


def _rope_tables(positions, tb):
    b, t = positions.shape
    half = RET_HEAD_DIM // 2
    inv_freq = ROPE_BASE ** (-jnp.arange(half, dtype=F32) / half)
    invf = jnp.concatenate([inv_freq, inv_freq]).reshape(1, RET_HEAD_DIM)
    pos3 = positions.reshape(b, t, 1)
    out = jax.ShapeDtypeStruct((b, t, RET_HEAD_DIM), F32)
    return pl.pallas_call(
        _rope_kernel,
        grid=(b, t // tb),
        in_specs=[pl.BlockSpec((1, tb, 1), lambda i, j: (i, j, 0)),
                  pl.BlockSpec((1, RET_HEAD_DIM), lambda i, j: (0, 0))],
        out_specs=[pl.BlockSpec((1, tb, RET_HEAD_DIM), lambda i, j: (i, j, 0)),
                   pl.BlockSpec((1, tb, RET_HEAD_DIM), lambda i, j: (i, j, 0))],
        out_shape=[out, out],
        name="rope_tables",
    )(pos3, invf)


N_MAIN_BLOCKS = OFF_GA // MXU_COLS
N_GATE_BLOCKS = N_BRANCHES * D_MODEL // MXU_COLS


def _repack_kernel(wt_ref, out_ref):
    c = pl.program_id(1)
    w = wt_ref[0].T
    lane = lax.broadcasted_iota(jnp.int32, w.shape, 1)
    keep = (c < N_MAIN_BLOCKS + N_GATE_BLOCKS) | (lane < GLA_GATE_RANK)
    out_ref[...] = jnp.where(keep, w, 0.0).astype(BF16)


def _repack_w_in(w_in):
    depth, d, _ = w_in.shape
    w_t = jnp.swapaxes(w_in, 1, 2)

    def first_feature(c):
        gates = OFF_GATES + (c - N_MAIN_BLOCKS) * MXU_COLS
        return jnp.where(c < N_MAIN_BLOCKS, c * MXU_COLS,
                         jnp.where(c < N_MAIN_BLOCKS + N_GATE_BLOCKS, gates, OFF_GA))

    return pl.pallas_call(
        _repack_kernel,
        grid=(depth, N_MAIN_BLOCKS + N_GATE_BLOCKS + 1),
        in_specs=[pl.BlockSpec((pl.Element(1), pl.Element(MXU_COLS), pl.Element(d)),
                               lambda l, c: (l, pl.multiple_of(first_feature(c), SUBLANES), 0))],
        out_specs=pl.BlockSpec((None, d, MXU_COLS), lambda l, c: (l, 0, c)),
        out_shape=jax.ShapeDtypeStruct((depth, d, W_COLS), BF16),
        compiler_params=pltpu.CompilerParams(dimension_semantics=("arbitrary", "arbitrary")),
        name="repack_w_in",
    )(w_t)


def _mixer_kernel(x_ref, xnext_ref, cos_ref, sin_ref, npre_ref, win_ref, convw_ref,
                  retgn_ref, wa2_ref, ba_ref, glagn_ref, wba_ref, wbb_ref, wbc_ref, wout_ref,
                  npost_ref, emat_ref, wg32_ref, wu32_ref, wd32_ref, out_ref, wg16_ref, wu16_ref, wd16_ref,
                  h_s, hn_s, pa_s, pb_s, pc_s, gate_s, ob_s, oc_s, m_s, zc_s, rstate_s, gstate_s, cum_s,
                  td_s, srep_s, dmask_s, rdec_s, *, tb):
    b_idx = pl.program_id(0)
    t_idx = pl.program_id(1)

    @pl.when((b_idx == 0) & (t_idx == 0))
    def _():
        ri = lax.broadcasted_iota(jnp.int32, (tb, tb), 0)
        ci = lax.broadcasted_iota(jnp.int32, (tb, tb), 1)
        dif = jnp.maximum(ri - ci, 0).astype(F32)
        rowf = lax.broadcasted_iota(jnp.int32, (tb, RET_HEAD_DIM), 0).astype(F32)
        for hh in range(RET_HEADS):
            log_g = math.log(1.0 - 2.0 ** (-5.0 - hh))
            dmask_s[hh] = jnp.where(ri >= ci, jnp.exp(log_g * dif), 0.0)
            rdec_s[2 * hh] = jnp.exp(log_g * (rowf + 1.0))
            rdec_s[2 * hh + 1] = jnp.exp(log_g * (tb - 1.0 - rowf))

    @pl.when(t_idx == 0)
    def _():
        zc_s[0:SUBLANES, :] = jnp.zeros((SUBLANES, CONV_DIM), F32)
        rstate_s[...] = jnp.zeros(rstate_s.shape, F32)
        gstate_s[...] = jnp.zeros(gstate_s.shape, F32)

    def proj_task(dst_ref, w_ref, w_lo, lo, act=None, h_ref=h_s):
        def run():
            r = _dot(h_ref[...], w_ref[:, w_lo + lo:w_lo + lo + MXU_COLS])
            dst_ref[:, lo:lo + MXU_COLS] = r if act is None else act(r)
        return run

    ret_q_cols = range(0, RET_DIM + MXU_COLS, MXU_COLS)

    @pl.when((b_idx == 0) & (t_idx == 0))
    def _():
        h_s[...] = _rms(x_ref[0], npre_ref[...]).astype(BF16)
        for lo in ret_q_cols:
            proj_task(pb_s, win_ref, OFF_RET, lo)()

    half_tb = tb // 2

    def next_h_task(r0):
        def run():
            xn = xnext_ref[0, r0:r0 + half_tb, :]
            hn_s[r0:r0 + half_tb, :] = _rms(xn, npre_ref[...]).astype(BF16)
        return run

    def ffn_cast_task():
        wg16_ref[...] = wg32_ref[...].astype(BF16)
        wu16_ref[...] = wu32_ref[...].astype(BF16)
        wd16_ref[...] = wd32_ref[...].astype(BF16)

    def interleave(primary, fillers):
        done = 0
        for i, step in enumerate(primary):
            step()
            while done * len(primary) < (i + 1) * len(fillers):
                fillers[done]()
                done += 1

    ret_proj = [proj_task(pb_s, win_ref, OFF_RET, lo)
                for lo in range(RET_DIM + MXU_COLS, 4 * RET_DIM, MXU_COLS)]
    conv_proj = [proj_task(pa_s, win_ref, OFF_CONV, lo) for lo in range(0, 3 * CONV_DIM, MXU_COLS)]
    gate_proj = [proj_task(gate_s, win_ref, W_GATES, lo, _sigmoid) for lo in range(0, N_BRANCHES * D_MODEL, MXU_COLS)]

    gc = GLA_CHUNK
    kd = GLA_KEY_DIM
    qscale = GLA_KEY_HEAD_DIM ** -0.5
    n_chunks = tb // gc
    gla_proj = [proj_task(pc_s, win_ref, OFF_GLA, lo) for lo in range(0, GLA_COLS, MXU_COLS)]
    ga_down = _dot(h_s[...], win_ref[:, W_GA:W_GA + LANES])
    logits = _dot(ga_down.astype(BF16), wa2_ref[...]) + ba_ref[...]
    for task in gla_proj[:3]:
        task()
    log_a = _log_sigmoid(logits) * (LOG2E / GLA_GATE_TAU)
    ri = lax.broadcasted_iota(jnp.int32, (gc, gc), 0)
    ci = lax.broadcasted_iota(jnp.int32, (gc, gc), 1)
    tril = jnp.where(ri >= ci, 1.0, 0.0).astype(BF16)
    for c in range(n_chunks):
        g = log_a[c * gc:(c + 1) * gc, :]
        g_hi = g.astype(BF16)
        g_lo = (g - g_hi.astype(F32)).astype(BF16)
        cum_s[c * gc:(c + 1) * gc, :] = _dot(tril, g_hi) + _dot(tril, g_lo)
    for task in gla_proj[3:]:
        task()

    def diag_unit(u):
        r0 = u * DIAG_TILE
        subs = []
        for rs in range(r0, r0 + DIAG_TILE, GLA_SUB):
            subs.append((rs, pc_s[rs:rs + GLA_SUB, 0:kd] * qscale, cum_s[rs:rs + GLA_SUB, :]))
        for j in range(GLA_SUB):
            tjs = []
            for rs, qb, cb in subs:
                kj = pc_s[rs + j:rs + j + 1, kd:2 * kd]
                cj = cum_s[rs + j:rs + j + 1, :]
                tjs.append(qb * kj * jnp.exp2(jnp.minimum(cb - cj, 0.0)))
            td_s[r0:r0 + DIAG_TILE, j * kd:(j + 1) * kd] = jnp.concatenate(tjs, axis=0).astype(BF16)

    interleave([functools.partial(diag_unit, u) for u in range(tb // DIAG_TILE)], ret_proj + conv_proj[:4])

    def conv_task():
        z = pa_s[:, 2 * CONV_DIM:3 * CONV_DIM] * pa_s[:, 0:CONV_DIM]
        zc_s[SUBLANES:SUBLANES + tb, :] = z
        z1 = zc_s[SUBLANES - 1:SUBLANES - 1 + tb, :]
        z2 = zc_s[SUBLANES - 2:SUBLANES - 2 + tb, :]
        cw = convw_ref[...]
        y_a = pa_s[:, CONV_DIM:2 * CONV_DIM] * (cw[0:1, :] * z2 + cw[1:2, :] * z1 + cw[2:3, :] * z)
        zc_s[0:SUBLANES, :] = z[tb - SUBLANES:tb, :]
        m_s[...] = gate_s[:, 0:D_MODEL] * _dot(y_a.astype(BF16), wba_ref[...])

    ret = [dict() for _ in range(RET_HEADS)]

    def ret_rotary(hh):
        lo = hh * RET_HEAD_DIM
        cosf = cos_ref[0]
        sins = sin_ref[0]
        qh = pb_s[:, lo:lo + RET_HEAD_DIM]
        kh = pb_s[:, RET_DIM + lo:RET_DIM + lo + RET_HEAD_DIM]
        qc = (qh * cosf + pltpu.roll(qh, RET_HEAD_DIM // 2, axis=1) * sins) * (RET_HEAD_DIM ** -0.5)
        kc = kh * cosf + pltpu.roll(kh, RET_HEAD_DIM // 2, axis=1) * sins
        ret[hh].update(q=qc.astype(BF16), k=kc.astype(BF16),
                       qd=(qc * rdec_s[2 * hh]).astype(BF16), kd=(kc * rdec_s[2 * hh + 1]).astype(BF16),
                       v=pb_s[:, 2 * RET_DIM + lo:2 * RET_DIM + lo + RET_HEAD_DIM].astype(BF16))

    def ret_scores(hh):
        r = ret[hh]
        log_g = math.log(1.0 - 2.0 ** (-5.0 - hh))
        state = rstate_s[hh]
        r["s"] = (_dot_nt(r["q"], r["k"]) * dmask_s[hh]).astype(BF16)
        r["inter"] = _dot(r["qd"], state.astype(BF16))
        rstate_s[hh] = state * math.exp(log_g * tb) + _dot_tn(r["kd"], r["v"])

    def ret_output(hh):
        r = ret[hh]
        lo = hh * RET_HEAD_DIM
        o = _dot(r["s"], r["v"]) + r["inter"]
        oc = o - jnp.mean(o, axis=-1, keepdims=True)
        ob_s[:, lo:lo + RET_HEAD_DIM] = oc * lax.rsqrt(jnp.mean(oc * oc, axis=-1, keepdims=True) + EPS)

    rowi = lax.broadcasted_iota(jnp.int32, (gc, LANES), 0)
    lanei = lax.broadcasted_iota(jnp.int32, (gc, LANES), 1)
    colj = lanei & (GLA_KEY_HEAD_DIM - 1)
    sub_shift = GLA_SUB.bit_length() - 1
    diag_mask = ((colj >> sub_shift) == (rowi >> sub_shift)) & (colj <= rowi)
    head_lo = lanei < GLA_KEY_HEAD_DIM
    srow = lax.broadcasted_iota(jnp.int32, (2 * GLA_VAL_HEAD_DIM, LANES), 0)
    slane = lax.broadcasted_iota(jnp.int32, (2 * GLA_VAL_HEAD_DIM, LANES), 1)
    state_mask = (srow < GLA_VAL_HEAD_DIM) == (slane < GLA_KEY_HEAD_DIM)
    zero_v = jnp.zeros((gc, GLA_VAL_HEAD_DIM), BF16)
    block_masks = {}
    half = gc // 4
    while half >= GLA_SUB:
        shift = (2 * half).bit_length() - 1
        block_masks[half] = (colj >> shift) == (rowi >> shift)
        half //= 2

    gla = [[dict() for _ in range(GLA_HEADS // 2)] for _ in range(n_chunks)]

    def gla_intra(c):
        rows = slice(c * gc, (c + 1) * gc)
        cum = cum_s[rows, :]
        last = cum[gc - 1:gc, :]
        q = pc_s[rows, 0:kd] * qscale
        k = pc_s[rows, kd:2 * kd]
        q_in = q * jnp.exp2(cum)
        k_st = k * jnp.exp2(last - cum)
        levels = []
        half = gc // 2
        while half >= GLA_SUB:
            a_rows, b_rows = [], []
            zero_half = jnp.zeros((half, kd), F32)
            for base in range(0, gc, 2 * half):
                mid = base + half
                ref = cum[mid - 1:mid, :]
                b_rows += [k[base:mid, :] * jnp.exp2(ref - cum[base:mid, :]), zero_half]
                a_rows += [zero_half, q[mid:mid + half, :] * jnp.exp2(cum[mid:mid + half, :] - ref)]
            levels.append((half, jnp.concatenate(a_rows, axis=0), jnp.concatenate(b_rows, axis=0)))
            half //= 2
        for pr in range(GLA_HEADS // 2):
            ls = slice(pr * LANES, (pr + 1) * LANES)
            s_off = None
            for half, a, b in levels:
                b_two = jnp.concatenate([jnp.where(head_lo, b[:, ls], 0.0).astype(BF16),
                                         jnp.where(head_lo, 0.0, b[:, ls]).astype(BF16)], axis=0)
                r = _dot_nt(a[:, ls].astype(BF16), b_two)
                if 2 * half < gc:
                    r = jnp.where(block_masks[half], r, 0.0)
                s_off = r if s_off is None else s_off + r
            s = jnp.where(diag_mask, srep_s[rows, ls], s_off).astype(BF16)
            v_lo = 2 * kd + 2 * pr * GLA_VAL_HEAD_DIM
            v0 = pc_s[rows, v_lo:v_lo + GLA_VAL_HEAD_DIM].astype(BF16)
            v1 = pc_s[rows, v_lo + GLA_VAL_HEAD_DIM:v_lo + 2 * GLA_VAL_HEAD_DIM].astype(BF16)
            v_bd = jnp.concatenate([jnp.concatenate([v0, zero_v], axis=1),
                                    jnp.concatenate([zero_v, v1], axis=1)], axis=0)
            v_pair = jnp.concatenate([v0, v1], axis=1)
            upd = _dot_tn(v_pair, k_st[:, ls].astype(BF16))
            gla[c][pr].update(intra=_dot(s, v_bd), q_in=q_in[:, ls].astype(BF16),
                              upd=jnp.where(state_mask, upd, 0.0), decay=jnp.exp2(last[:, ls]))

    def gla_carry():
        for pr in range(GLA_HEADS // 2):
            st = gstate_s[pr]
            for c in range(n_chunks):
                g = gla[c][pr]
                o = g["intra"] + _dot_nt(g["q_in"], st.astype(BF16))
                st = st * g["decay"] + g["upd"]
                for hl in range(2):
                    oh = o[:, hl * GLA_VAL_HEAD_DIM:(hl + 1) * GLA_VAL_HEAD_DIM]
                    lo = (2 * pr + hl) * GLA_VAL_HEAD_DIM
                    oc_s[c * gc:(c + 1) * gc, lo:lo + GLA_VAL_HEAD_DIM] = (
                        oh * lax.rsqrt(jnp.mean(oh * oh, axis=-1, keepdims=True) + EPS))
            gstate_s[pr] = st

    def srep_task():
        srep_s[...] = _dot(td_s[...], emat_ref[...])

    heads = range(RET_HEADS)
    interleave([next_h_task(0), next_h_task(half_tb), ffn_cast_task]
               + [functools.partial(ret_rotary, hh) for hh in heads], conv_proj[4:])
    interleave([srep_task] + [functools.partial(ret_scores, hh) for hh in heads], gate_proj[:4])
    interleave([functools.partial(gla_intra, c) for c in range(n_chunks)], gate_proj[4:8])
    interleave([functools.partial(ret_output, hh) for hh in heads], gate_proj[8:])
    h_s[...] = hn_s[...]
    interleave([gla_carry], [conv_task])

    y_b = ob_s[...] * retgn_ref[...] * _silu(pb_s[:, 3 * RET_DIM:4 * RET_DIM])
    m_s[...] += gate_s[:, D_MODEL:2 * D_MODEL] * _dot(y_b.astype(BF16), wbb_ref[...])
    y_c = oc_s[...] * glagn_ref[...] * _silu(pc_s[:, 2 * kd + GLA_VAL_DIM:2 * kd + 2 * GLA_VAL_DIM])
    merged = m_s[...] + gate_s[:, 2 * D_MODEL:3 * D_MODEL] * _dot(y_c.astype(BF16), wbc_ref[...])

    merged_b = merged.astype(BF16)
    sumsq = jnp.zeros((tb, 1), F32)
    for lo in range(0, D_MODEL, MXU_COLS):
        blk = _dot(merged_b, wout_ref[:, lo:lo + MXU_COLS])
        m_s[:, lo:lo + MXU_COLS] = blk
        sumsq = sumsq + jnp.sum(blk * blk, axis=-1, keepdims=True)
    for lo in ret_q_cols:
        proj_task(pb_s, win_ref, OFF_RET, lo, h_ref=hn_s)()
    scale = lax.rsqrt(sumsq * (1.0 / D_MODEL) + EPS)
    out_ref[0] = x_ref[0] + m_s[...] * scale * npost_ref[...]


def _layer_spec(arr, layer, grid_rank):
    index = (lambda i: (layer, 0, 0)) if grid_rank == 1 else (lambda i, j: (layer, 0, 0))
    return pl.BlockSpec((None,) + arr.shape[1:], index, pipeline_mode=pl.Buffered(1))


def _diag_reduce_matrix():
    kidx = np.arange(GLA_SUB * GLA_KEY_DIM)
    j = kidx // GLA_KEY_DIM
    hk = (kidx % GLA_KEY_DIM) // GLA_KEY_HEAD_DIM
    col = np.arange(GLA_KEY_DIM)
    hc = col // GLA_KEY_HEAD_DIM
    cj = (col % GLA_KEY_HEAD_DIM) % GLA_SUB
    return jnp.asarray((hk[:, None] == hc[None, :]) & (j[:, None] == cj[None, :]), dtype=BF16)


def _slab_spec(arr, layer, n_steps, nt, stacked):
    rows, cols = arr.shape[-2:]
    n_slabs = 1
    while n_slabs * 2 <= n_steps and rows % (n_slabs * 2 * DIAG_TILE) == 0:
        n_slabs *= 2
    per = n_steps // n_slabs
    slab = lambda i, j: jnp.minimum((i * nt + j) // per, n_slabs - 1)
    if stacked:
        return pl.BlockSpec((None, rows // n_slabs, cols), lambda i, j: (layer, slab(i, j), 0))
    return pl.BlockSpec((rows // n_slabs, cols), lambda i, j: (slab(i, j), 0))


def _mixer(x, cos_t, sin_t, layer_params, emat, ffn_weights, layer, tb):
    b, t, d = x.shape
    nt = t // tb
    n_steps = b * nt
    tok = lambda width: pl.BlockSpec((1, tb, width), lambda i, j: (i, j, 0))
    next_tok = pl.BlockSpec((1, tb, d), lambda i, j: (jnp.minimum(i + (j + 1) // nt, b - 1), (j + 1) % nt, 0))
    const_specs = [_layer_spec(p, layer, 2) for p in layer_params]
    const_specs.append(pl.BlockSpec(emat.shape, lambda i, j: (0, 0), pipeline_mode=pl.Buffered(1)))
    scratch = [
        pltpu.VMEM((tb, D_MODEL), BF16),
        pltpu.VMEM((tb, D_MODEL), BF16),
        pltpu.VMEM((tb, 3 * CONV_DIM), F32),
        pltpu.VMEM((tb, 4 * RET_DIM), F32),
        pltpu.VMEM((tb, GLA_COLS), F32),
        pltpu.VMEM((tb, N_BRANCHES * D_MODEL), F32),
        pltpu.VMEM((tb, RET_DIM), F32),
        pltpu.VMEM((tb, GLA_VAL_DIM), F32),
        pltpu.VMEM((tb, D_MODEL), F32),
        pltpu.VMEM((tb + SUBLANES, CONV_DIM), F32),
        pltpu.VMEM((RET_HEADS, RET_HEAD_DIM, RET_HEAD_DIM), F32),
        pltpu.VMEM((GLA_HEADS // 2, 2 * GLA_VAL_HEAD_DIM, LANES), F32),
        pltpu.VMEM((tb, GLA_KEY_DIM), F32),
        pltpu.VMEM((tb, GLA_SUB * GLA_KEY_DIM), BF16),
        pltpu.VMEM((tb, GLA_KEY_DIM), F32),
        pltpu.VMEM((RET_HEADS, tb, tb), F32),
        pltpu.VMEM((2 * RET_HEADS, tb, RET_HEAD_DIM), F32),
    ]
    return pl.pallas_call(
        functools.partial(_mixer_kernel, tb=tb),
        grid=(b, t // tb),
        in_specs=[tok(d), next_tok, tok(RET_HEAD_DIM), tok(RET_HEAD_DIM)] + const_specs
        + [_slab_spec(w, layer, n_steps, nt, True) for w in ffn_weights],
        out_specs=[tok(d)] + [_slab_spec(w, layer, n_steps, nt, False) for w in ffn_weights],
        out_shape=[jax.ShapeDtypeStruct(x.shape, x.dtype)]
        + [jax.ShapeDtypeStruct(w.shape[1:], BF16) for w in ffn_weights],
        scratch_shapes=scratch,
        compiler_params=pltpu.CompilerParams(dimension_semantics=("arbitrary", "arbitrary"),
                                             vmem_limit_bytes=VMEM_LIMIT_BYTES),
        name="mixer",
    )(x, x, cos_t, sin_t, *layer_params, emat, *ffn_weights)


def _ffn_kernel(x_ref, npre_ref, wg_ref, wu_ref, wd_ref, npost_ref, out_ref, *, fc):
    tm = x_ref.shape[0]
    hm = tm // 2
    halves = (slice(0, hm), slice(hm, tm))
    h_halves = [_rms(x_ref[rows, :], npre_ref[...]).astype(BF16) for rows in halves]
    chunks = list(range(0, D_FF, fc))
    assert len(chunks) >= 2 and D_FF % fc == 0

    def hidden(h, f):
        g = _dot(h, wg_ref[:, f:f + fc])
        u = _dot(h, wu_ref[:, f:f + fc])
        return (_silu(g) * u).astype(BF16)

    f0 = chunks[0]
    acc = jnp.concatenate([_dot(hidden(h, f0), wd_ref[f0:f0 + fc, :]) for h in h_halves], axis=0)
    h = jnp.concatenate(h_halves, axis=0)
    for f in chunks[1:-1]:
        acc = acc + _dot(hidden(h, f), wd_ref[f:f + fc, :])
    f1 = chunks[-1]
    for rows, h_half in zip(halves, h_halves):
        total = acc[rows, :] + _dot(hidden(h_half, f1), wd_ref[f1:f1 + fc, :])
        out_ref[rows, :] = x_ref[rows, :] + _rms(total, npost_ref[...])


def _ffn(x2, npre, weights, npost, layer, tm, fc):
    n, d = x2.shape
    resident = lambda w: pl.BlockSpec(w.shape, lambda i: (0, 0), pipeline_mode=pl.Buffered(1))
    return pl.pallas_call(
        functools.partial(_ffn_kernel, fc=fc),
        grid=(n // tm,),
        in_specs=[pl.BlockSpec((tm, d), lambda i: (i, 0)), _layer_spec(npre, layer, 1)]
        + [resident(w) for w in weights] + [_layer_spec(npost, layer, 1)],
        out_specs=pl.BlockSpec((tm, d), lambda i: (i, 0)),
        out_shape=jax.ShapeDtypeStruct(x2.shape, x2.dtype),
        compiler_params=pltpu.CompilerParams(dimension_semantics=("arbitrary",),
                                             vmem_limit_bytes=VMEM_LIMIT_BYTES),
        name="swiglu",
    )(x2, npre, *weights, npost)


def _pick_block(t, want):
    tb = min(want, t)
    while t % tb:
        tb //= 2
    return tb


def kernel(x, positions, norm_mix_pre, w_in, conv_w, ret_gn_w, gla_w_a2, gla_b_a, gla_gn_w, w_branch_a, w_branch_b, w_branch_c, w_out, norm_mix_post, norm_ffn_pre, w_ffn_gate, w_ffn_up, w_ffn_down, norm_ffn_post):
    b, t, d = x.shape
    depth = w_in.shape[0]
    tb = _pick_block(t, MIXER_BLOCK)
    tm = _pick_block(b * t, FFN_BLOCK)
    cos_t, sin_t = _rope_tables(positions, _pick_block(t, ROPE_BLOCK))
    emat = _diag_reduce_matrix()
    rows = lambda v: v.reshape(depth, 1, -1)
    pad_rank = LANES - GLA_GATE_RANK
    w_packed = _repack_w_in(w_in)
    wa2 = jnp.pad(gla_w_a2, ((0, 0), (0, pad_rank), (0, 0))).astype(BF16)
    mixer_params = [rows(norm_mix_pre), w_packed, conv_w, rows(ret_gn_w), wa2, rows(gla_b_a), rows(gla_gn_w),
                    w_branch_a.astype(BF16), w_branch_b.astype(BF16), w_branch_c.astype(BF16),
                    w_out.astype(BF16), rows(norm_mix_post)]
    ffn_weights = [w_ffn_gate, w_ffn_up, w_ffn_down]
    for l in range(depth):
        x, *ffn_bf16 = _mixer(x, cos_t, sin_t, mixer_params, emat, ffn_weights, l, tb)
        x = _ffn(x.reshape(b * t, d), rows(norm_ffn_pre), ffn_bf16, rows(norm_ffn_post), l, tm,
                 MXU_COLS).reshape(b, t, d)
    return x
```

```python
import functools
import math

import jax
import jax.numpy as jnp
import numpy as np
from jax import lax
from jax.experimental import pallas as pl
from jax.experimental.pallas import tpu as pltpu

D_MODEL = 1024
CONV_DIM = 512
CONV_WIDTH = 3
RET_HEADS = 4
RET_HEAD_DIM = 128
RET_DIM = RET_HEADS * RET_HEAD_DIM
GLA_HEADS = 4
GLA_KEY_HEAD_DIM = 64
GLA_VAL_HEAD_DIM = 128
GLA_KEY_DIM = GLA_HEADS * GLA_KEY_HEAD_DIM
GLA_VAL_DIM = GLA_HEADS * GLA_VAL_HEAD_DIM
GLA_GATE_RANK = 16
GLA_GATE_TAU = 16.0
N_BRANCHES = 3
D_FF = 2816
ROPE_BASE = 10000.0
EPS = 1e-6
LOG2E = 1.0 / math.log(2.0)

OFF_CONV = 0
OFF_RET = 3 * CONV_DIM
OFF_GLA = OFF_RET + 4 * RET_DIM
OFF_GA = OFF_GLA + 2 * GLA_KEY_DIM + 2 * GLA_VAL_DIM
OFF_GATES = OFF_GA + GLA_GATE_RANK
GLA_COLS = 2 * GLA_KEY_DIM + 2 * GLA_VAL_DIM

LANES = 128
W_GATES = OFF_GA
W_GA = W_GATES + N_BRANCHES * D_MODEL
SUBLANES = 8
MXU_COLS = 256
REPACK_BLOCK = 1024
W_COLS = W_GA + REPACK_BLOCK
GLA_CHUNK = 64
GLA_SUB = 8
DIAG_TILE = 16
VMEM_LIMIT_BYTES = 56 * 1024 * 1024
MIXER_BLOCK = 256
FFN_BLOCK = 1024
ROPE_BLOCK = 2048

BF16 = jnp.bfloat16
F32 = jnp.float32


def _dot(a, b):
    return jnp.dot(a, b, preferred_element_type=F32)


def _dot_nt(a, b):
    return lax.dot_general(a, b, (((1,), (1,)), ((), ())), preferred_element_type=F32)


def _dot_tn(a, b):
    return lax.dot_general(a, b, (((0,), (0,)), ((), ())), preferred_element_type=F32)


def _rms(x, w):
    return x * lax.rsqrt(jnp.mean(x * x, axis=-1, keepdims=True) + EPS) * w


def _sigmoid(x):
    return 1.0 / (1.0 + jnp.exp2(x * -LOG2E))


def _silu(x):
    return x * _sigmoid(x)


def _log_sigmoid(x):
    return jnp.minimum(x, 0.0) - jnp.log(1.0 + jnp.exp2(jnp.abs(x) * -LOG2E))


def _rope_kernel(pos_ref, invf_ref, cos_ref, sin_ref):
    tb = pos_ref.shape[1]
    hb = tb // 2
    half = RET_HEAD_DIM // 2
    pos = pos_ref[0].astype(F32)
    lo = lax.broadcasted_iota(jnp.int32, (hb, RET_HEAD_DIM), 1) < half
    ang = jnp.where(lo, pos[0:hb], pos[hb:tb]) * invf_ref[...]
    c = jnp.cos(ang)
    s = jnp.sin(ang)
    c_sw = pltpu.roll(c, half, axis=1)
    s_sw = pltpu.roll(s, half, axis=1)
    sign = jnp.where(lo, -1.0, 1.0)
    cos_ref[0, 0:hb, :] = jnp.where(lo, c, c_sw)
    cos_ref[0, hb:tb, :] = jnp.where(lo, c_sw, c)
    sin_ref[0, 0:hb, :] = jnp.where(lo, s, s_sw) * sign
    sin_ref[0, hb:tb, :] = jnp.where(lo, s_sw, s) * sign


def _rope_tables(positions, tb):
    b, t = positions.shape
    half = RET_HEAD_DIM // 2
    inv_freq = ROPE_BASE ** (-jnp.arange(half, dtype=F32) / half)
    invf = jnp.concatenate([inv_freq, inv_freq]).reshape(1, RET_HEAD_DIM)
    pos3 = positions.reshape(b, t, 1)
    out = jax.ShapeDtypeStruct((b, t, RET_HEAD_DIM), F32)
    return pl.pallas_call(
        _rope_kernel,
        grid=(b, t // tb),
        in_specs=[pl.BlockSpec((1, tb, 1), lambda i, j: (i, j, 0)),
                  pl.BlockSpec((1, RET_HEAD_DIM), lambda i, j: (0, 0))],
        out_specs=[pl.BlockSpec((1, tb, RET_HEAD_DIM), lambda i, j: (i, j, 0)),
                   pl.BlockSpec((1, tb, RET_HEAD_DIM), lambda i, j: (i, j, 0))],
        out_shape=[out, out],
        name="rope_tables",
    )(pos3, invf)


N_MAIN_BLOCKS = OFF_GA // REPACK_BLOCK
N_GATE_BLOCKS = N_BRANCHES * D_MODEL // REPACK_BLOCK


def _repack_kernel(wt_ref, out_ref):
    c = pl.program_id(1)
    w = wt_ref[0].T
    lane = lax.broadcasted_iota(jnp.int32, w.shape, 1)
    keep = (c < N_MAIN_BLOCKS + N_GATE_BLOCKS) | (lane < GLA_GATE_RANK)
    out_ref[...] = jnp.where(keep, w, 0.0).astype(BF16)


def _repack_w_in(w_in):
    depth, d, _ = w_in.shape
    w_t = jnp.swapaxes(w_in, 1, 2)

    def first_feature(c):
        gates = OFF_GATES + (c - N_MAIN_BLOCKS) * REPACK_BLOCK
        return jnp.where(c < N_MAIN_BLOCKS, c * REPACK_BLOCK,
                         jnp.where(c < N_MAIN_BLOCKS + N_GATE_BLOCKS, gates, OFF_GA))

    return pl.pallas_call(
        _repack_kernel,
        grid=(depth, N_MAIN_BLOCKS + N_GATE_BLOCKS + 1),
        in_specs=[pl.BlockSpec((pl.Element(1), pl.Element(REPACK_BLOCK), pl.Element(d)),
                               lambda l, c: (l, pl.multiple_of(first_feature(c), SUBLANES), 0))],
        out_specs=pl.BlockSpec((None, d, REPACK_BLOCK), lambda l, c: (l, 0, c)),
        out_shape=jax.ShapeDtypeStruct((depth, d, W_COLS), BF16),
        compiler_params=pltpu.CompilerParams(dimension_semantics=("arbitrary", "arbitrary"),
                                             vmem_limit_bytes=VMEM_LIMIT_BYTES),
        name="repack_w_in",
    )(w_t)


def _mixer_kernel(x_ref, xnext_ref, cos_ref, sin_ref, win_ref, npre_ref, convw_ref,
                  retgn_ref, wa2_ref, ba_ref, glagn_ref, wba_ref, wbb_ref, wbc_ref, wout_ref,
                  npost_ref, emat_ref, wg32_ref, wu32_ref, wd32_ref, out_ref, wg16_ref, wu16_ref, wd16_ref,
                  h_s, hn_s, pa_s, pb_s, pc_s, gate_s, ob_s, oc_s, m_s, zc_s, rstate_s, gstate_s, cum_s,
                  td_s, srep_s, dmask_s, rdec_s, *, tb):
    b_idx = pl.program_id(0)
    t_idx = pl.program_id(1)

    @pl.when((b_idx == 0) & (t_idx == 0))
    def _():
        ri = lax.broadcasted_iota(jnp.int32, (tb, tb), 0)
        ci = lax.broadcasted_iota(jnp.int32, (tb, tb), 1)
        dif = jnp.maximum(ri - ci, 0).astype(F32)
        rowf = lax.broadcasted_iota(jnp.int32, (tb, RET_HEAD_DIM), 0).astype(F32)
        for hh in range(RET_HEADS):
            log_g = math.log(1.0 - 2.0 ** (-5.0 - hh))
            dmask_s[hh] = jnp.where(ri >= ci, jnp.exp(log_g * dif), 0.0)
            rdec_s[2 * hh] = jnp.exp(log_g * (rowf + 1.0))
            rdec_s[2 * hh + 1] = jnp.exp(log_g * (tb - 1.0 - rowf))

    @pl.when(t_idx == 0)
    def _():
        zc_s[0:SUBLANES, :] = jnp.zeros((SUBLANES, CONV_DIM), F32)
        rstate_s[...] = jnp.zeros(rstate_s.shape, F32)
        gstate_s[...] = jnp.zeros(gstate_s.shape, F32)

    def proj_task(dst_ref, w_ref, w_lo, lo, act=None, h_ref=h_s):
        def run():
            r = _dot(h_ref[...], w_ref[:, w_lo + lo:w_lo + lo + MXU_COLS])
            dst_ref[:, lo:lo + MXU_COLS] = r if act is None else act(r)
        return run

    ret_q_cols = range(0, RET_DIM + MXU_COLS, MXU_COLS)

    @pl.when((b_idx == 0) & (t_idx == 0))
    def _():
        h_s[...] = _rms(x_ref[0], npre_ref[...]).astype(BF16)
        for lo in ret_q_cols:
            proj_task(pb_s, win_ref, OFF_RET, lo)()

    half_tb = tb // 2

    def next_h_task(r0):
        def run():
            xn = xnext_ref[0, r0:r0 + half_tb, :]
            hn_s[r0:r0 + half_tb, :] = _rms(xn, npre_ref[...]).astype(BF16)
        return run

    def ffn_cast_task():
        wg16_ref[...] = wg32_ref[...].astype(BF16)
        wu16_ref[...] = wu32_ref[...].astype(BF16)
        wd16_ref[...] = wd32_ref[...].astype(BF16)

    def interleave(primary, fillers):
        done = 0
        for i, step in enumerate(primary):
            step()
            while done * len(primary) < (i + 1) * len(fillers):
                fillers[done]()
                done += 1

    ret_proj = [proj_task(pb_s, win_ref, OFF_RET, lo)
                for lo in range(RET_DIM + MXU_COLS, 4 * RET_DIM, MXU_COLS)]
    conv_proj = [proj_task(pa_s, win_ref, OFF_CONV, lo) for lo in range(0, 3 * CONV_DIM, MXU_COLS)]
    gate_proj = [proj_task(gate_s, win_ref, W_GATES, lo, _sigmoid) for lo in range(0, N_BRANCHES * D_MODEL, MXU_COLS)]

    gc = GLA_CHUNK
    kd = GLA_KEY_DIM
    qscale = GLA_KEY_HEAD_DIM ** -0.5
    n_chunks = tb // gc
    gla_proj = [proj_task(pc_s, win_ref, OFF_GLA, lo) for lo in range(0, GLA_COLS, MXU_COLS)]
    ga_down = _dot(h_s[...], win_ref[:, W_GA:W_GA + LANES])
    logits = _dot(ga_down.astype(BF16), wa2_ref[...]) + ba_ref[...]
    for task in gla_proj[:3]:
        task()
    log_a = _log_sigmoid(logits) * (LOG2E / GLA_GATE_TAU)
    ri = lax.broadcasted_iota(jnp.int32, (gc, gc), 0)
    ci = lax.broadcasted_iota(jnp.int32, (gc, gc), 1)
    tril = jnp.where(ri >= ci, 1.0, 0.0).astype(BF16)
    for c in range(n_chunks):
        g = log_a[c * gc:(c + 1) * gc, :]
        g_hi = g.astype(BF16)
        g_lo = (g - g_hi.astype(F32)).astype(BF16)
        cum_s[c * gc:(c + 1) * gc, :] = _dot(tril, g_hi) + _dot(tril, g_lo)
    for task in gla_proj[3:]:
        task()

    def diag_unit(u):
        r0 = u * DIAG_TILE
        subs = []
        for rs in range(r0, r0 + DIAG_TILE, GLA_SUB):
            subs.append((rs, pc_s[rs:rs + GLA_SUB, 0:kd] * qscale, cum_s[rs:rs + GLA_SUB, :]))
        for j in range(GLA_SUB):
            tjs = []
            for rs, qb, cb in subs:
                kj = pc_s[rs + j:rs + j + 1, kd:2 * kd]
                cj = cum_s[rs + j:rs + j + 1, :]
                tjs.append(qb * kj * jnp.exp2(jnp.minimum(cb - cj, 0.0)))
            td_s[r0:r0 + DIAG_TILE, j * kd:(j + 1) * kd] = jnp.concatenate(tjs, axis=0).astype(BF16)

    interleave([functools.partial(diag_unit, u) for u in range(tb // DIAG_TILE)], ret_proj + conv_proj[:4])

    def conv_task():
        z = pa_s[:, 2 * CONV_DIM:3 * CONV_DIM] * pa_s[:, 0:CONV_DIM]
        zc_s[SUBLANES:SUBLANES + tb, :] = z
        z1 = zc_s[SUBLANES - 1:SUBLANES - 1 + tb, :]
        z2 = zc_s[SUBLANES - 2:SUBLANES - 2 + tb, :]
        cw = convw_ref[...]
        y_a = pa_s[:, CONV_DIM:2 * CONV_DIM] * (cw[0:1, :] * z2 + cw[1:2, :] * z1 + cw[2:3, :] * z)
        zc_s[0:SUBLANES, :] = z[tb - SUBLANES:tb, :]
        m_s[...] = gate_s[:, 0:D_MODEL] * _dot(y_a.astype(BF16), wba_ref[...])

    ret = [dict() for _ in range(RET_HEADS)]

    def ret_rotary(hh):
        lo = hh * RET_HEAD_DIM
        cosf = cos_ref[0]
        sins = sin_ref[0]
        qh = pb_s[:, lo:lo + RET_HEAD_DIM]
        kh = pb_s[:, RET_DIM + lo:RET_DIM + lo + RET_HEAD_DIM]
        qc = (qh * cosf + pltpu.roll(qh, RET_HEAD_DIM // 2, axis=1) * sins) * (RET_HEAD_DIM ** -0.5)
        kc = kh * cosf + pltpu.roll(kh, RET_HEAD_DIM // 2, axis=1) * sins
        ret[hh].update(q=qc.astype(BF16), k=kc.astype(BF16),
                       qd=(qc * rdec_s[2 * hh]).astype(BF16), kd=(kc * rdec_s[2 * hh + 1]).astype(BF16),
                       v=pb_s[:, 2 * RET_DIM + lo:2 * RET_DIM + lo + RET_HEAD_DIM].astype(BF16))

    def ret_scores(hh):
        r = ret[hh]
        log_g = math.log(1.0 - 2.0 ** (-5.0 - hh))
        state = rstate_s[hh]
        r["s"] = (_dot_nt(r["q"], r["k"]) * dmask_s[hh]).astype(BF16)
        r["inter"] = _dot(r["qd"], state.astype(BF16))
        rstate_s[hh] = state * math.exp(log_g * tb) + _dot_tn(r["kd"], r["v"])

    def ret_output(hh):
        r = ret[hh]
        lo = hh * RET_HEAD_DIM
        o = _dot(r["s"], r["v"]) + r["inter"]
        oc = o - jnp.mean(o, axis=-1, keepdims=True)
        ob_s[:, lo:lo + RET_HEAD_DIM] = oc * lax.rsqrt(jnp.mean(oc * oc, axis=-1, keepdims=True) + EPS)

    rowi = lax.broadcasted_iota(jnp.int32, (gc, LANES), 0)
    lanei = lax.broadcasted_iota(jnp.int32, (gc, LANES), 1)
    colj = lanei & (GLA_KEY_HEAD_DIM - 1)
    sub_shift = GLA_SUB.bit_length() - 1
    diag_mask = ((colj >> sub_shift) == (rowi >> sub_shift)) & (colj <= rowi)
    head_lo = lanei < GLA_KEY_HEAD_DIM
    srow = lax.broadcasted_iota(jnp.int32, (2 * GLA_VAL_HEAD_DIM, LANES), 0)
    slane = lax.broadcasted_iota(jnp.int32, (2 * GLA_VAL_HEAD_DIM, LANES), 1)
    state_mask = (srow < GLA_VAL_HEAD_DIM) == (slane < GLA_KEY_HEAD_DIM)
    zero_v = jnp.zeros((gc, GLA_VAL_HEAD_DIM), BF16)
    block_masks = {}
    half = gc // 4
    while half >= GLA_SUB:
        shift = (2 * half).bit_length() - 1
        block_masks[half] = (colj >> shift) == (rowi >> shift)
        half //= 2

    gla = [[dict() for _ in range(GLA_HEADS // 2)] for _ in range(n_chunks)]

    def gla_intra(c):
        rows = slice(c * gc, (c + 1) * gc)
        cum = cum_s[rows, :]
        last = cum[gc - 1:gc, :]
        q = pc_s[rows, 0:kd] * qscale
        k = pc_s[rows, kd:2 * kd]
        q_in = q * jnp.exp2(cum)
        k_st = k * jnp.exp2(last - cum)
        levels = []
        half = gc // 2
        while half >= GLA_SUB:
            a_rows, b_rows = [], []
            zero_half = jnp.zeros((half, kd), F32)
            for base in range(0, gc, 2 * half):
                mid = base + half
                ref = cum[mid - 1:mid, :]
                b_rows += [k[base:mid, :] * jnp.exp2(ref - cum[base:mid, :]), zero_half]
                a_rows += [zero_half, q[mid:mid + half, :] * jnp.exp2(cum[mid:mid + half, :] - ref)]
            levels.append((half, jnp.concatenate(a_rows, axis=0), jnp.concatenate(b_rows, axis=0)))
            half //= 2
        for pr in range(GLA_HEADS // 2):
            ls = slice(pr * LANES, (pr + 1) * LANES)
            s_off = None
            for half, a, b in levels:
                b_two = jnp.concatenate([jnp.where(head_lo, b[:, ls], 0.0).astype(BF16),
                                         jnp.where(head_lo, 0.0, b[:, ls]).astype(BF16)], axis=0)
                r = _dot_nt(a[:, ls].astype(BF16), b_two)
                if 2 * half < gc:
                    r = jnp.where(block_masks[half], r, 0.0)
                s_off = r if s_off is None else s_off + r
            s = jnp.where(diag_mask, srep_s[rows, ls], s_off).astype(BF16)
            v_lo = 2 * kd + 2 * pr * GLA_VAL_HEAD_DIM
            v0 = pc_s[rows, v_lo:v_lo + GLA_VAL_HEAD_DIM].astype(BF16)
            v1 = pc_s[rows, v_lo + GLA_VAL_HEAD_DIM:v_lo + 2 * GLA_VAL_HEAD_DIM].astype(BF16)
            v_bd = jnp.concatenate([jnp.concatenate([v0, zero_v], axis=1),
                                    jnp.concatenate([zero_v, v1], axis=1)], axis=0)
            v_pair = jnp.concatenate([v0, v1], axis=1)
            upd = _dot_tn(v_pair, k_st[:, ls].astype(BF16))
            gla[c][pr].update(intra=_dot(s, v_bd), q_in=q_in[:, ls].astype(BF16),
                              upd=jnp.where(state_mask, upd, 0.0), decay=jnp.exp2(last[:, ls]))

    def gla_carry():
        for pr in range(GLA_HEADS // 2):
            st = gstate_s[pr]
            for c in range(n_chunks):
                g = gla[c][pr]
                o = g["intra"] + _dot_nt(g["q_in"], st.astype(BF16))
                st = st * g["decay"] + g["upd"]
                for hl in range(2):
                    oh = o[:, hl * GLA_VAL_HEAD_DIM:(hl + 1) * GLA_VAL_HEAD_DIM]
                    lo = (2 * pr + hl) * GLA_VAL_HEAD_DIM
                    oc_s[c * gc:(c + 1) * gc, lo:lo + GLA_VAL_HEAD_DIM] = (
                        oh * lax.rsqrt(jnp.mean(oh * oh, axis=-1, keepdims=True) + EPS))
            gstate_s[pr] = st

    def srep_task():
        srep_s[...] = _dot(td_s[...], emat_ref[...])

    heads = range(RET_HEADS)
    interleave([next_h_task(0), next_h_task(half_tb), ffn_cast_task]
               + [functools.partial(ret_rotary, hh) for hh in heads], conv_proj[4:])
    interleave([srep_task] + [functools.partial(ret_scores, hh) for hh in heads], gate_proj[:4])
    interleave([functools.partial(gla_intra, c) for c in range(n_chunks)], gate_proj[4:8])
    interleave([functools.partial(ret_output, hh) for hh in heads], gate_proj[8:])
    h_s[...] = hn_s[...]
    interleave([gla_carry], [conv_task])

    y_b = ob_s[...] * retgn_ref[...] * _silu(pb_s[:, 3 * RET_DIM:4 * RET_DIM])
    m_s[...] += gate_s[:, D_MODEL:2 * D_MODEL] * _dot(y_b.astype(BF16), wbb_ref[...])
    y_c = oc_s[...] * glagn_ref[...] * _silu(pc_s[:, 2 * kd + GLA_VAL_DIM:2 * kd + 2 * GLA_VAL_DIM])
    merged = m_s[...] + gate_s[:, 2 * D_MODEL:3 * D_MODEL] * _dot(y_c.astype(BF16), wbc_ref[...])

    merged_b = merged.astype(BF16)
    sumsq = jnp.zeros((tb, 1), F32)
    for lo in range(0, D_MODEL, MXU_COLS):
        blk = _dot(merged_b, wout_ref[:, lo:lo + MXU_COLS])
        m_s[:, lo:lo + MXU_COLS] = blk
        sumsq = sumsq + jnp.sum(blk * blk, axis=-1, keepdims=True)
    for lo in ret_q_cols:
        proj_task(pb_s, win_ref, OFF_RET, lo, h_ref=hn_s)()
    scale = lax.rsqrt(sumsq * (1.0 / D_MODEL) + EPS)
    out_ref[0] = x_ref[0] + m_s[...] * scale * npost_ref[...]


def _layer_spec(arr, layer, grid_rank):
    index = (lambda i: (layer, 0, 0)) if grid_rank == 1 else (lambda i, j: (layer, 0, 0))
    return pl.BlockSpec((None,) + arr.shape[1:], index, pipeline_mode=pl.Buffered(1))


def _diag_reduce_matrix():
    kidx = np.arange(GLA_SUB * GLA_KEY_DIM)
    j = kidx // GLA_KEY_DIM
    hk = (kidx % GLA_KEY_DIM) // GLA_KEY_HEAD_DIM
    col = np.arange(GLA_KEY_DIM)
    hc = col // GLA_KEY_HEAD_DIM
    cj = (col % GLA_KEY_HEAD_DIM) % GLA_SUB
    return jnp.asarray((hk[:, None] == hc[None, :]) & (j[:, None] == cj[None, :]), dtype=BF16)


def _slab_spec(arr, layer, n_steps, nt, stacked):
    rows, cols = arr.shape[-2:]
    n_slabs = 1
    while n_slabs * 2 <= n_steps and rows % (n_slabs * 2 * DIAG_TILE) == 0:
        n_slabs *= 2
    per = n_steps // n_slabs
    slab = lambda i, j: jnp.minimum((i * nt + j) // per, n_slabs - 1)
    if stacked:
        return pl.BlockSpec((None, rows // n_slabs, cols), lambda i, j: (layer, slab(i, j), 0))
    return pl.BlockSpec((rows // n_slabs, cols), lambda i, j: (slab(i, j), 0))


def _mixer(x, cos_t, sin_t, w_packed, layer_params, emat, ffn_weights, layer, tb):
    b, t, d = x.shape
    nt = t // tb
    n_steps = b * nt
    tok = lambda width: pl.BlockSpec((1, tb, width), lambda i, j: (i, j, 0))
    next_tok = pl.BlockSpec((1, tb, d), lambda i, j: (jnp.minimum(i + (j + 1) // nt, b - 1), (j + 1) % nt, 0))
    const_specs = [pl.BlockSpec((None, d, W_GA + LANES), lambda i, j: (layer, 0, 0), pipeline_mode=pl.Buffered(1))]
    const_specs += [_layer_spec(p, layer, 2) for p in layer_params]
    const_specs.append(pl.BlockSpec(emat.shape, lambda i, j: (0, 0), pipeline_mode=pl.Buffered(1)))
    scratch = [
        pltpu.VMEM((tb, D_MODEL), BF16),
        pltpu.VMEM((tb, D_MODEL), BF16),
        pltpu.VMEM((tb, 3 * CONV_DIM), F32),
        pltpu.VMEM((tb, 4 * RET_DIM), F32),
        pltpu.VMEM((tb, GLA_COLS), F32),
        pltpu.VMEM((tb, N_BRANCHES * D_MODEL), F32),
        pltpu.VMEM((tb, RET_DIM), F32),
        pltpu.VMEM((tb, GLA_VAL_DIM), F32),
        pltpu.VMEM((tb, D_MODEL), F32),
        pltpu.VMEM((tb + SUBLANES, CONV_DIM), F32),
        pltpu.VMEM((RET_HEADS, RET_HEAD_DIM, RET_HEAD_DIM), F32),
        pltpu.VMEM((GLA_HEADS // 2, 2 * GLA_VAL_HEAD_DIM, LANES), F32),
        pltpu.VMEM((tb, GLA_KEY_DIM), F32),
        pltpu.VMEM((tb, GLA_SUB * GLA_KEY_DIM), BF16),
        pltpu.VMEM((tb, GLA_KEY_DIM), F32),
        pltpu.VMEM((RET_HEADS, tb, tb), F32),
        pltpu.VMEM((2 * RET_HEADS, tb, RET_HEAD_DIM), F32),
    ]
    return pl.pallas_call(
        functools.partial(_mixer_kernel, tb=tb),
        grid=(b, t // tb),
        in_specs=[tok(d), next_tok, tok(RET_HEAD_DIM), tok(RET_HEAD_DIM)] + const_specs
        + [_slab_spec(w, layer, n_steps, nt, True) for w in ffn_weights],
        out_specs=[tok(d)] + [_slab_spec(w, layer, n_steps, nt, False) for w in ffn_weights],
        out_shape=[jax.ShapeDtypeStruct(x.shape, x.dtype)]
        + [jax.ShapeDtypeStruct(w.shape[1:], BF16) for w in ffn_weights],
        scratch_shapes=scratch,
        compiler_params=pltpu.CompilerParams(dimension_semantics=("arbitrary", "arbitrary"),
                                             vmem_limit_bytes=VMEM_LIMIT_BYTES),
        name="mixer",
    )(x, x, cos_t, sin_t, w_packed, *layer_params, emat, *ffn_weights)


def _ffn_kernel(x_ref, npre_ref, wg_ref, wu_ref, wd_ref, npost_ref, out_ref, *, fc):
    tm = x_ref.shape[0]
    hm = tm // 2
    halves = (slice(0, hm), slice(hm, tm))
    h_halves = [_rms(x_ref[rows, :], npre_ref[...]).astype(BF16) for rows in halves]
    chunks = list(range(0, D_FF, fc))
    assert len(chunks) >= 2 and D_FF % fc == 0

    def hidden(h, f):
        g = _dot(h, wg_ref[:, f:f + fc])
        u = _dot(h, wu_ref[:, f:f + fc])
        return (_silu(g) * u).astype(BF16)

    f0 = chunks[0]
    acc = jnp.concatenate([_dot(hidden(h, f0), wd_ref[f0:f0 + fc, :]) for h in h_halves], axis=0)
    h = jnp.concatenate(h_halves, axis=0)
    for f in chunks[1:-1]:
        acc = acc + _dot(hidden(h, f), wd_ref[f:f + fc, :])
    f1 = chunks[-1]
    for rows, h_half in zip(halves, h_halves):
        total = acc[rows, :] + _dot(hidden(h_half, f1), wd_ref[f1:f1 + fc, :])
        out_ref[rows, :] = x_ref[rows, :] + _rms(total, npost_ref[...])


def _ffn(x2, npre, weights, npost, layer, tm, fc):
    n, d = x2.shape
    resident = lambda w: pl.BlockSpec(w.shape, lambda i: (0, 0), pipeline_mode=pl.Buffered(1))
    return pl.pallas_call(
        functools.partial(_ffn_kernel, fc=fc),
        grid=(n // tm,),
        in_specs=[pl.BlockSpec((tm, d), lambda i: (i, 0)), _layer_spec(npre, layer, 1)]
        + [resident(w) for w in weights] + [_layer_spec(npost, layer, 1)],
        out_specs=pl.BlockSpec((tm, d), lambda i: (i, 0)),
        out_shape=jax.ShapeDtypeStruct(x2.shape, x2.dtype),
        compiler_params=pltpu.CompilerParams(dimension_semantics=("arbitrary",),
                                             vmem_limit_bytes=VMEM_LIMIT_BYTES),
        name="swiglu",
    )(x2, npre, *weights, npost)


def _pick_block(t, want):
    tb = min(want, t)
    while t % tb:
        tb //= 2
    return tb


def kernel(x, positions, norm_mix_pre, w_in, conv_w, ret_gn_w, gla_w_a2, gla_b_a, gla_gn_w, w_branch_a, w_branch_b, w_branch_c, w_out, norm_mix_post, norm_ffn_pre, w_ffn_gate, w_ffn_up, w_ffn_down, norm_ffn_post):
    b, t, d = x.shape
    depth = w_in.shape[0]
    tb = _pick_block(t, MIXER_BLOCK)
    tm = _pick_block(b * t, FFN_BLOCK)
    cos_t, sin_t = _rope_tables(positions, _pick_block(t, ROPE_BLOCK))
    emat = _diag_reduce_matrix()
    rows = lambda v: v.reshape(depth, 1, -1)
    pad_rank = LANES - GLA_GATE_RANK
    w_packed = _repack_w_in(w_in)
    wa2 = jnp.pad(gla_w_a2, ((0, 0), (0, pad_rank), (0, 0))).astype(BF16)
    mixer_params = [rows(norm_mix_pre), conv_w, rows(ret_gn_w), wa2, rows(gla_b_a), rows(gla_gn_w),
                    w_branch_a.astype(BF16), w_branch_b.astype(BF16), w_branch_c.astype(BF16),
                    w_out.astype(BF16), rows(norm_mix_post)]
    ffn_weights = [w_ffn_gate, w_ffn_up, w_ffn_down]
    for l in range(depth):
        x, *ffn_bf16 = _mixer(x, cos_t, sin_t, w_packed, mixer_params, emat, ffn_weights, l, tb)
        x = _ffn(x.reshape(b * t, d), rows(norm_ffn_pre), ffn_bf16, rows(norm_ffn_post), l, tm,
                 MXU_COLS).reshape(b, t, d)
    return x
```

```python
import functools
import math

import jax
import jax.numpy as jnp
import numpy as np
from jax import lax
from jax.experimental import pallas as pl
from jax.experimental.pallas import tpu as pltpu

D_MODEL = 1024
CONV_DIM = 512
CONV_WIDTH = 3
RET_HEADS = 4
RET_HEAD_DIM = 128
RET_DIM = RET_HEADS * RET_HEAD_DIM
GLA_HEADS = 4
GLA_KEY_HEAD_DIM = 64
GLA_VAL_HEAD_DIM = 128
GLA_KEY_DIM = GLA_HEADS * GLA_KEY_HEAD_DIM
GLA_VAL_DIM = GLA_HEADS * GLA_VAL_HEAD_DIM
GLA_GATE_RANK = 16
GLA_GATE_TAU = 16.0
N_BRANCHES = 3
D_FF = 2816
ROPE_BASE = 10000.0
EPS = 1e-6
LOG2E = 1.0 / math.log(2.0)

OFF_CONV = 0
OFF_RET = 3 * CONV_DIM
OFF_GLA = OFF_RET + 4 * RET_DIM
OFF_GA = OFF_GLA + 2 * GLA_KEY_DIM + 2 * GLA_VAL_DIM
OFF_GATES = OFF_GA + GLA_GATE_RANK
GLA_COLS = 2 * GLA_KEY_DIM + 2 * GLA_VAL_DIM

LANES = 128
W_GATES = OFF_GA
W_GA = W_GATES + N_BRANCHES * D_MODEL
SUBLANES = 8
MXU_COLS = 256
REPACK_BLOCK = 1024
W_COLS = W_GA + REPACK_BLOCK
GLA_CHUNK = 64
GLA_SUB = 8
DIAG_TILE = 16
VMEM_LIMIT_BYTES = 56 * 1024 * 1024
MIXER_BLOCK = 256
FFN_BLOCK = 1024
ROPE_BLOCK = 2048

BF16 = jnp.bfloat16
F32 = jnp.float32


def _dot(a, b):
    return jnp.dot(a, b, preferred_element_type=F32)


def _dot_nt(a, b):
    return lax.dot_general(a, b, (((1,), (1,)), ((), ())), preferred_element_type=F32)


def _dot_tn(a, b):
    return lax.dot_general(a, b, (((0,), (0,)), ((), ())), preferred_element_type=F32)


def _rms(x, w):
    return x * lax.rsqrt(jnp.mean(x * x, axis=-1, keepdims=True) + EPS) * w


def _sigmoid(x):
    return 1.0 / (1.0 + jnp.exp2(x * -LOG2E))


def _silu(x):
    return x * _sigmoid(x)


def _log_sigmoid(x):
    return jnp.minimum(x, 0.0) - jnp.log(1.0 + jnp.exp2(jnp.abs(x) * -LOG2E))


def _rope_kernel(pos_ref, invf_ref, cos_ref, sin_ref):
    tb = pos_ref.shape[1]
    hb = tb // 2
    half = RET_HEAD_DIM // 2
    pos = pos_ref[0].astype(F32)
    lo = lax.broadcasted_iota(jnp.int32, (hb, RET_HEAD_DIM), 1) < half
    ang = jnp.where(lo, pos[0:hb], pos[hb:tb]) * invf_ref[...]
    c = jnp.cos(ang)
    s = jnp.sin(ang)
    c_sw = pltpu.roll(c, half, axis=1)
    s_sw = pltpu.roll(s, half, axis=1)
    sign = jnp.where(lo, -1.0, 1.0)
    cos_ref[0, 0:hb, :] = jnp.where(lo, c, c_sw)
    cos_ref[0, hb:tb, :] = jnp.where(lo, c_sw, c)
    sin_ref[0, 0:hb, :] = jnp.where(lo, s, s_sw) * sign
    sin_ref[0, hb:tb, :] = jnp.where(lo, s_sw, s) * sign


def _rope_tables(positions, tb):
    b, t = positions.shape
    half = RET_HEAD_DIM // 2
    inv_freq = ROPE_BASE ** (-jnp.arange(half, dtype=F32) / half)
    invf = jnp.concatenate([inv_freq, inv_freq]).reshape(1, RET_HEAD_DIM)
    pos3 = positions.reshape(b, t, 1)
    out = jax.ShapeDtypeStruct((b, t, RET_HEAD_DIM), F32)
    return pl.pallas_call(
        _rope_kernel,
        grid=(b, t // tb),
        in_specs=[pl.BlockSpec((1, tb, 1), lambda i, j: (i, j, 0)),
                  pl.BlockSpec((1, RET_HEAD_DIM), lambda i, j: (0, 0))],
        out_specs=[pl.BlockSpec((1, tb, RET_HEAD_DIM), lambda i, j: (i, j, 0)),
                   pl.BlockSpec((1, tb, RET_HEAD_DIM), lambda i, j: (i, j, 0))],
        out_shape=[out, out],
        name="rope_tables",
    )(pos3, invf)


N_MAIN_BLOCKS = OFF_GA // REPACK_BLOCK
N_GATE_BLOCKS = N_BRANCHES * D_MODEL // REPACK_BLOCK


def _repack_kernel(wt_ref, out_ref):
    c = pl.program_id(1)
    w = wt_ref[0].T
    lane = lax.broadcasted_iota(jnp.int32, w.shape, 1)
    keep = (c < N_MAIN_BLOCKS + N_GATE_BLOCKS) | (lane < GLA_GATE_RANK)
    out_ref[...] = jnp.where(keep, w, 0.0).astype(BF16)


def _prep_kernel(wt_ref, pos_ref, invf_ref, out_ref, cos_ref, sin_ref, *, n_rope_units):
    _repack_kernel(wt_ref, out_ref)

    @pl.when(pl.program_id(0) * pl.num_programs(1) + pl.program_id(1) < n_rope_units)
    def _():
        _rope_kernel(pos_ref, invf_ref, cos_ref, sin_ref)


def _prep(w_in, positions):
    depth, d, _ = w_in.shape
    b, t = positions.shape
    w_t = jnp.swapaxes(w_in, 1, 2)
    n_blocks = N_MAIN_BLOCKS + N_GATE_BLOCKS + 1
    n_steps = depth * n_blocks

    def first_feature(c):
        gates = OFF_GATES + (c - N_MAIN_BLOCKS) * REPACK_BLOCK
        return jnp.where(c < N_MAIN_BLOCKS, c * REPACK_BLOCK,
                         jnp.where(c < N_MAIN_BLOCKS + N_GATE_BLOCKS, gates, OFF_GA))

    w_in_spec = pl.BlockSpec((pl.Element(1), pl.Element(REPACK_BLOCK), pl.Element(d)),
                             lambda l, c: (l, pl.multiple_of(first_feature(c), SUBLANES), 0))
    w_out_spec = pl.BlockSpec((None, d, REPACK_BLOCK), lambda l, c: (l, 0, c))
    w_out_shape = jax.ShapeDtypeStruct((depth, d, W_COLS), BF16)
    params = pltpu.CompilerParams(dimension_semantics=("arbitrary", "arbitrary"), vmem_limit_bytes=VMEM_LIMIT_BYTES)

    k = 1
    while b * k * 2 <= n_steps and t % (k * 2 * DIAG_TILE) == 0:
        k *= 2
    if b * k > n_steps:
        w_packed = pl.pallas_call(_repack_kernel, grid=(depth, n_blocks), in_specs=[w_in_spec],
                                  out_specs=w_out_spec, out_shape=w_out_shape, compiler_params=params,
                                  name="repack_w_in")(w_t)
        return (w_packed,) + tuple(_rope_tables(positions, _pick_block(t, ROPE_BLOCK)))

    rb = t // k
    n_units = b * k

    def unit(l, c):
        u = jnp.minimum(l * n_blocks + c, n_units - 1)
        return (u // k, u % k, 0)

    half = RET_HEAD_DIM // 2
    inv_freq = ROPE_BASE ** (-jnp.arange(half, dtype=F32) / half)
    invf = jnp.concatenate([inv_freq, inv_freq]).reshape(1, RET_HEAD_DIM)
    table = jax.ShapeDtypeStruct((b, t, RET_HEAD_DIM), F32)
    return pl.pallas_call(
        functools.partial(_prep_kernel, n_rope_units=n_units),
        grid=(depth, n_blocks),
        in_specs=[w_in_spec, pl.BlockSpec((1, rb, 1), unit), pl.BlockSpec((1, RET_HEAD_DIM), lambda l, c: (0, 0))],
        out_specs=[w_out_spec, pl.BlockSpec((1, rb, RET_HEAD_DIM), unit), pl.BlockSpec((1, rb, RET_HEAD_DIM), unit)],
        out_shape=[w_out_shape, table, table],
        compiler_params=params,
        name="prep",
    )(w_t, positions.reshape(b, t, 1), invf)


def _mixer_kernel(x_ref, xnext_ref, cos_ref, sin_ref, win_ref, npre_ref, convw_ref,
                  retgn_ref, wa2_ref, ba_ref, glagn_ref, wba_ref, wbb_ref, wbc_ref, wout_ref,
                  npost_ref, emat_ref, wg32_ref, wu32_ref, wd32_ref, out_ref, wg16_ref, wu16_ref, wd16_ref,
                  h_s, hn_s, pa_s, pb_s, pc_s, gate_s, ob_s, oc_s, m_s, zc_s, rstate_s, gstate_s, cum_s,
                  td_s, srep_s, dmask_s, rdec_s, *, tb):
    b_idx = pl.program_id(0)
    t_idx = pl.program_id(1)

    @pl.when((b_idx == 0) & (t_idx == 0))
    def _():
        ri = lax.broadcasted_iota(jnp.int32, (tb, tb), 0)
        ci = lax.broadcasted_iota(jnp.int32, (tb, tb), 1)
        dif = jnp.maximum(ri - ci, 0).astype(F32)
        rowf = lax.broadcasted_iota(jnp.int32, (tb, RET_HEAD_DIM), 0).astype(F32)
        for hh in range(RET_HEADS):
            log_g = math.log(1.0 - 2.0 ** (-5.0 - hh))
            dmask_s[hh] = jnp.where(ri >= ci, jnp.exp(log_g * dif), 0.0)
            rdec_s[2 * hh] = jnp.exp(log_g * (rowf + 1.0))
            rdec_s[2 * hh + 1] = jnp.exp(log_g * (tb - 1.0 - rowf))

    @pl.when(t_idx == 0)
    def _():
        zc_s[0:SUBLANES, :] = jnp.zeros((SUBLANES, CONV_DIM), F32)
        rstate_s[...] = jnp.zeros(rstate_s.shape, F32)
        gstate_s[...] = jnp.zeros(gstate_s.shape, F32)

    def proj_task(dst_ref, w_ref, w_lo, lo, act=None, h_ref=h_s):
        def run():
            r = _dot(h_ref[...], w_ref[:, w_lo + lo:w_lo + lo + MXU_COLS])
            dst_ref[:, lo:lo + MXU_COLS] = r if act is None else act(r)
        return run

    ret_q_cols = range(0, RET_DIM + MXU_COLS, MXU_COLS)

    @pl.when((b_idx == 0) & (t_idx == 0))
    def _():
        h_s[...] = _rms(x_ref[0], npre_ref[...]).astype(BF16)
        for lo in ret_q_cols:
            proj_task(pb_s, win_ref, OFF_RET, lo)()

    half_tb = tb // 2

    def next_h_task(r0):
        def run():
            xn = xnext_ref[0, r0:r0 + half_tb, :]
            hn_s[r0:r0 + half_tb, :] = _rms(xn, npre_ref[...]).astype(BF16)
        return run

    def ffn_cast_task():
        wg16_ref[...] = wg32_ref[...].astype(BF16)
        wu16_ref[...] = wu32_ref[...].astype(BF16)
        wd16_ref[...] = wd32_ref[...].astype(BF16)

    def interleave(primary, fillers):
        done = 0
        for i, step in enumerate(primary):
            step()
            while done * len(primary) < (i + 1) * len(fillers):
                fillers[done]()
                done += 1

    ret_proj = [proj_task(pb_s, win_ref, OFF_RET, lo)
                for lo in range(RET_DIM + MXU_COLS, 4 * RET_DIM, MXU_COLS)]
    conv_proj = [proj_task(pa_s, win_ref, OFF_CONV, lo) for lo in range(0, 3 * CONV_DIM, MXU_COLS)]
    gate_proj = [proj_task(gate_s, win_ref, W_GATES, lo, _sigmoid) for lo in range(0, N_BRANCHES * D_MODEL, MXU_COLS)]

    gc = GLA_CHUNK
    kd = GLA_KEY_DIM
    qscale = GLA_KEY_HEAD_DIM ** -0.5
    n_chunks = tb // gc
    gla_proj = [proj_task(pc_s, win_ref, OFF_GLA, lo) for lo in range(0, GLA_COLS, MXU_COLS)]
    ga_down = _dot(h_s[...], win_ref[:, W_GA:W_GA + LANES])
    logits = _dot(ga_down.astype(BF16), wa2_ref[...]) + ba_ref[...]
    for task in gla_proj[:3]:
        task()
    log_a = _log_sigmoid(logits) * (LOG2E / GLA_GATE_TAU)
    ri = lax.broadcasted_iota(jnp.int32, (gc, gc), 0)
    ci = lax.broadcasted_iota(jnp.int32, (gc, gc), 1)
    tril = jnp.where(ri >= ci, 1.0, 0.0).astype(BF16)
    for c in range(n_chunks):
        g = log_a[c * gc:(c + 1) * gc, :]
        g_hi = g.astype(BF16)
        g_lo = (g - g_hi.astype(F32)).astype(BF16)
        cum_s[c * gc:(c + 1) * gc, :] = _dot(tril, g_hi) + _dot(tril, g_lo)
    for task in gla_proj[3:]:
        task()

    def diag_unit(u):
        r0 = u * DIAG_TILE
        subs = []
        for rs in range(r0, r0 + DIAG_TILE, GLA_SUB):
            subs.append((rs, pc_s[rs:rs + GLA_SUB, 0:kd] * qscale, cum_s[rs:rs + GLA_SUB, :]))
        for j in range(GLA_SUB):
            tjs = []
            for rs, qb, cb in subs:
                kj = pc_s[rs + j:rs + j + 1, kd:2 * kd]
                cj = cum_s[rs + j:rs + j + 1, :]
                tjs.append(qb * kj * jnp.exp2(jnp.minimum(cb - cj, 0.0)))
            td_s[r0:r0 + DIAG_TILE, j * kd:(j + 1) * kd] = jnp.concatenate(tjs, axis=0).astype(BF16)

    interleave([functools.partial(diag_unit, u) for u in range(tb // DIAG_TILE)], ret_proj + conv_proj[:4])

    def conv_task():
        z = pa_s[:, 2 * CONV_DIM:3 * CONV_DIM] * pa_s[:, 0:CONV_DIM]
        zc_s[SUBLANES:SUBLANES + tb, :] = z
        z1 = zc_s[SUBLANES - 1:SUBLANES - 1 + tb, :]
        z2 = zc_s[SUBLANES - 2:SUBLANES - 2 + tb, :]
        cw = convw_ref[...]
        y_a = pa_s[:, CONV_DIM:2 * CONV_DIM] * (cw[0:1, :] * z2 + cw[1:2, :] * z1 + cw[2:3, :] * z)
        zc_s[0:SUBLANES, :] = z[tb - SUBLANES:tb, :]
        m_s[...] = gate_s[:, 0:D_MODEL] * _dot(y_a.astype(BF16), wba_ref[...])

    ret = [dict() for _ in range(RET_HEADS)]

    def ret_rotary(hh):
        lo = hh * RET_HEAD_DIM
        cosf = cos_ref[0]
        sins = sin_ref[0]
        qh = pb_s[:, lo:lo + RET_HEAD_DIM]
        kh = pb_s[:, RET_DIM + lo:RET_DIM + lo + RET_HEAD_DIM]
        qc = (qh * cosf + pltpu.roll(qh, RET_HEAD_DIM // 2, axis=1) * sins) * (RET_HEAD_DIM ** -0.5)
        kc = kh * cosf + pltpu.roll(kh, RET_HEAD_DIM // 2, axis=1) * sins
        ret[hh].update(q=qc.astype(BF16), k=kc.astype(BF16),
                       qd=(qc * rdec_s[2 * hh]).astype(BF16), kd=(kc * rdec_s[2 * hh + 1]).astype(BF16),
                       v=pb_s[:, 2 * RET_DIM + lo:2 * RET_DIM + lo + RET_HEAD_DIM].astype(BF16))

    def ret_scores(hh):
        r = ret[hh]
        log_g = math.log(1.0 - 2.0 ** (-5.0 - hh))
        state = rstate_s[hh]
        r["s"] = (_dot_nt(r["q"], r["k"]) * dmask_s[hh]).astype(BF16)
        r["inter"] = _dot(r["qd"], state.astype(BF16))
        rstate_s[hh] = state * math.exp(log_g * tb) + _dot_tn(r["kd"], r["v"])

    def ret_output(hh):
        r = ret[hh]
        lo = hh * RET_HEAD_DIM
        o = _dot(r["s"], r["v"]) + r["inter"]
        oc = o - jnp.mean(o, axis=-1, keepdims=True)
        ob_s[:, lo:lo + RET_HEAD_DIM] = oc * lax.rsqrt(jnp.mean(oc * oc, axis=-1, keepdims=True) + EPS)

    rowi = lax.broadcasted_iota(jnp.int32, (gc, LANES), 0)
    lanei = lax.broadcasted_iota(jnp.int32, (gc, LANES), 1)
    colj = lanei & (GLA_KEY_HEAD_DIM - 1)
    sub_shift = GLA_SUB.bit_length() - 1
    diag_mask = ((colj >> sub_shift) == (rowi >> sub_shift)) & (colj <= rowi)
    head_lo = lanei < GLA_KEY_HEAD_DIM
    srow = lax.broadcasted_iota(jnp.int32, (2 * GLA_VAL_HEAD_DIM, LANES), 0)
    slane = lax.broadcasted_iota(jnp.int32, (2 * GLA_VAL_HEAD_DIM, LANES), 1)
    state_mask = (srow < GLA_VAL_HEAD_DIM) == (slane < GLA_KEY_HEAD_DIM)
    zero_v = jnp.zeros((gc, GLA_VAL_HEAD_DIM), BF16)
    block_masks = {}
    half = gc // 4
    while half >= GLA_SUB:
        shift = (2 * half).bit_length() - 1
        block_masks[half] = (colj >> shift) == (rowi >> shift)
        half //= 2

    gla = [[dict() for _ in range(GLA_HEADS // 2)] for _ in range(n_chunks)]

    def gla_intra(c):
        rows = slice(c * gc, (c + 1) * gc)
        cum = cum_s[rows, :]
        last = cum[gc - 1:gc, :]
        q = pc_s[rows, 0:kd] * qscale
        k = pc_s[rows, kd:2 * kd]
        q_in = q * jnp.exp2(cum)
        k_st = k * jnp.exp2(last - cum)
        levels = []
        half = gc // 2
        while half >= GLA_SUB:
            a_rows, b_rows = [], []
            zero_half = jnp.zeros((half, kd), F32)
            for base in range(0, gc, 2 * half):
                mid = base + half
                ref = cum[mid - 1:mid, :]
                b_rows += [k[base:mid, :] * jnp.exp2(ref - cum[base:mid, :]), zero_half]
                a_rows += [zero_half, q[mid:mid + half, :] * jnp.exp2(cum[mid:mid + half, :] - ref)]
            levels.append((half, jnp.concatenate(a_rows, axis=0), jnp.concatenate(b_rows, axis=0)))
            half //= 2
        for pr in range(GLA_HEADS // 2):
            ls = slice(pr * LANES, (pr + 1) * LANES)
            s_off = None
            for half, a, b in levels:
                b_two = jnp.concatenate([jnp.where(head_lo, b[:, ls], 0.0).astype(BF16),
                                         jnp.where(head_lo, 0.0, b[:, ls]).astype(BF16)], axis=0)
                r = _dot_nt(a[:, ls].astype(BF16), b_two)
                if 2 * half < gc:
                    r = jnp.where(block_masks[half], r, 0.0)
                s_off = r if s_off is None else s_off + r
            s = jnp.where(diag_mask, srep_s[rows, ls], s_off).astype(BF16)
            v_lo = 2 * kd + 2 * pr * GLA_VAL_HEAD_DIM
            v0 = pc_s[rows, v_lo:v_lo + GLA_VAL_HEAD_DIM].astype(BF16)
            v1 = pc_s[rows, v_lo + GLA_VAL_HEAD_DIM:v_lo + 2 * GLA_VAL_HEAD_DIM].astype(BF16)
            v_bd = jnp.concatenate([jnp.concatenate([v0, zero_v], axis=1),
                                    jnp.concatenate([zero_v, v1], axis=1)], axis=0)
            v_pair = jnp.concatenate([v0, v1], axis=1)
            upd = _dot_tn(v_pair, k_st[:, ls].astype(BF16))
            gla[c][pr].update(intra=_dot(s, v_bd), q_in=q_in[:, ls].astype(BF16),
                              upd=jnp.where(state_mask, upd, 0.0), decay=jnp.exp2(last[:, ls]))

    def gla_carry():
        for pr in range(GLA_HEADS // 2):
            st = gstate_s[pr]
            for c in range(n_chunks):
                g = gla[c][pr]
                o = g["intra"] + _dot_nt(g["q_in"], st.astype(BF16))
                st = st * g["decay"] + g["upd"]
                for hl in range(2):
                    oh = o[:, hl * GLA_VAL_HEAD_DIM:(hl + 1) * GLA_VAL_HEAD_DIM]
                    lo = (2 * pr + hl) * GLA_VAL_HEAD_DIM
                    oc_s[c * gc:(c + 1) * gc, lo:lo + GLA_VAL_HEAD_DIM] = (
                        oh * lax.rsqrt(jnp.mean(oh * oh, axis=-1, keepdims=True) + EPS))
            gstate_s[pr] = st

    def srep_task():
        srep_s[...] = _dot(td_s[...], emat_ref[...])

    heads = range(RET_HEADS)
    interleave([next_h_task(0), next_h_task(half_tb), ffn_cast_task]
               + [functools.partial(ret_rotary, hh) for hh in heads], conv_proj[4:])
    interleave([srep_task] + [functools.partial(ret_scores, hh) for hh in heads], gate_proj[:4])
    interleave([functools.partial(gla_intra, c) for c in range(n_chunks)], gate_proj[4:8])
    interleave([functools.partial(ret_output, hh) for hh in heads], gate_proj[8:])
    h_s[...] = hn_s[...]
    interleave([gla_carry], [conv_task])

    y_b = ob_s[...] * retgn_ref[...] * _silu(pb_s[:, 3 * RET_DIM:4 * RET_DIM])
    m_s[...] += gate_s[:, D_MODEL:2 * D_MODEL] * _dot(y_b.astype(BF16), wbb_ref[...])
    y_c = oc_s[...] * glagn_ref[...] * _silu(pc_s[:, 2 * kd + GLA_VAL_DIM:2 * kd + 2 * GLA_VAL_DIM])
    merged = m_s[...] + gate_s[:, 2 * D_MODEL:3 * D_MODEL] * _dot(y_c.astype(BF16), wbc_ref[...])

    merged_b = merged.astype(BF16)
    sumsq = jnp.zeros((tb, 1), F32)
    for lo in range(0, D_MODEL, MXU_COLS):
        blk = _dot(merged_b, wout_ref[:, lo:lo + MXU_COLS])
        m_s[:, lo:lo + MXU_COLS] = blk
        sumsq = sumsq + jnp.sum(blk * blk, axis=-1, keepdims=True)
    for lo in ret_q_cols:
        proj_task(pb_s, win_ref, OFF_RET, lo, h_ref=hn_s)()
    scale = lax.rsqrt(sumsq * (1.0 / D_MODEL) + EPS)
    out_ref[0] = x_ref[0] + m_s[...] * scale * npost_ref[...]


def _layer_spec(arr, layer, grid_rank):
    index = (lambda i: (layer, 0, 0)) if grid_rank == 1 else (lambda i, j: (layer, 0, 0))
    return pl.BlockSpec((None,) + arr.shape[1:], index, pipeline_mode=pl.Buffered(1))


def _diag_reduce_matrix():
    kidx = np.arange(GLA_SUB * GLA_KEY_DIM)
    j = kidx // GLA_KEY_DIM
    hk = (kidx % GLA_KEY_DIM) // GLA_KEY_HEAD_DIM
    col = np.arange(GLA_KEY_DIM)
    hc = col // GLA_KEY_HEAD_DIM
    cj = (col % GLA_KEY_HEAD_DIM) % GLA_SUB
    return jnp.asarray((hk[:, None] == hc[None, :]) & (j[:, None] == cj[None, :]), dtype=BF16)


def _slab_spec(arr, layer, n_steps, nt, stacked):
    rows, cols = arr.shape[-2:]
    n_slabs = 1
    while n_slabs * 2 <= n_steps and rows % (n_slabs * 2 * DIAG_TILE) == 0:
        n_slabs *= 2
    per = n_steps // n_slabs
    slab = lambda i, j: jnp.minimum((i * nt + j) // per, n_slabs - 1)
    if stacked:
        return pl.BlockSpec((None, rows // n_slabs, cols), lambda i, j: (layer, slab(i, j), 0))
    return pl.BlockSpec((rows // n_slabs, cols), lambda i, j: (slab(i, j), 0))


def _mixer(x, cos_t, sin_t, w_packed, layer_params, emat, ffn_weights, layer, tb):
    b, t, d = x.shape
    nt = t // tb
    n_steps = b * nt
    tok = lambda width: pl.BlockSpec((1, tb, width), lambda i, j: (i, j, 0))
    next_tok = pl.BlockSpec((1, tb, d), lambda i, j: (jnp.minimum(i + (j + 1) // nt, b - 1), (j + 1) % nt, 0))
    const_specs = [pl.BlockSpec((None, d, W_GA + LANES), lambda i, j: (layer, 0, 0), pipeline_mode=pl.Buffered(1))]
    const_specs += [_layer_spec(p, layer, 2) for p in layer_params]
    const_specs.append(pl.BlockSpec(emat.shape, lambda i, j: (0, 0), pipeline_mode=pl.Buffered(1)))
    scratch = [
        pltpu.VMEM((tb, D_MODEL), BF16),
        pltpu.VMEM((tb, D_MODEL), BF16),
        pltpu.VMEM((tb, 3 * CONV_DIM), F32),
        pltpu.VMEM((tb, 4 * RET_DIM), F32),
        pltpu.VMEM((tb, GLA_COLS), F32),
        pltpu.VMEM((tb, N_BRANCHES * D_MODEL), F32),
        pltpu.VMEM((tb, RET_DIM), F32),
        pltpu.VMEM((tb, GLA_VAL_DIM), F32),
        pltpu.VMEM((tb, D_MODEL), F32),
        pltpu.VMEM((tb + SUBLANES, CONV_DIM), F32),
        pltpu.VMEM((RET_HEADS, RET_HEAD_DIM, RET_HEAD_DIM), F32),
        pltpu.VMEM((GLA_HEADS // 2, 2 * GLA_VAL_HEAD_DIM, LANES), F32),
        pltpu.VMEM((tb, GLA_KEY_DIM), F32),
        pltpu.VMEM((tb, GLA_SUB * GLA_KEY_DIM), BF16),
        pltpu.VMEM((tb, GLA_KEY_DIM), F32),
        pltpu.VMEM((RET_HEADS, tb, tb), F32),
        pltpu.VMEM((2 * RET_HEADS, tb, RET_HEAD_DIM), F32),
    ]
    return pl.pallas_call(
        functools.partial(_mixer_kernel, tb=tb),
        grid=(b, t // tb),
        in_specs=[tok(d), next_tok, tok(RET_HEAD_DIM), tok(RET_HEAD_DIM)] + const_specs
        + [_slab_spec(w, layer, n_steps, nt, True) for w in ffn_weights],
        out_specs=[tok(d)] + [_slab_spec(w, layer, n_steps, nt, False) for w in ffn_weights],
        out_shape=[jax.ShapeDtypeStruct(x.shape, x.dtype)]
        + [jax.ShapeDtypeStruct(w.shape[1:], BF16) for w in ffn_weights],
        scratch_shapes=scratch,
        compiler_params=pltpu.CompilerParams(dimension_semantics=("arbitrary", "arbitrary"),
                                             vmem_limit_bytes=VMEM_LIMIT_BYTES),
        name="mixer",
    )(x, x, cos_t, sin_t, w_packed, *layer_params, emat, *ffn_weights)


def _ffn_kernel(x_ref, npre_ref, wg_ref, wu_ref, wd_ref, npost_ref, out_ref, *, fc):
    tm = x_ref.shape[0]
    hm = tm // 2
    halves = (slice(0, hm), slice(hm, tm))
    h_halves = [_rms(x_ref[rows, :], npre_ref[...]).astype(BF16) for rows in halves]
    chunks = list(range(0, D_FF, fc))
    assert len(chunks) >= 2 and D_FF % fc == 0

    def hidden(h, f):
        g = _dot(h, wg_ref[:, f:f + fc])
        u = _dot(h, wu_ref[:, f:f + fc])
        return (_silu(g) * u).astype(BF16)

    f0 = chunks[0]
    acc = jnp.concatenate([_dot(hidden(h, f0), wd_ref[f0:f0 + fc, :]) for h in h_halves], axis=0)
    h = jnp.concatenate(h_halves, axis=0)
    for f in chunks[1:-1]:
        acc = acc + _dot(hidden(h, f), wd_ref[f:f + fc, :])
    f1 = chunks[-1]
    for rows, h_half in zip(halves, h_halves):
        total = acc[rows, :] + _dot(hidden(h_half, f1), wd_ref[f1:f1 + fc, :])
        out_ref[rows, :] = x_ref[rows, :] + _rms(total, npost_ref[...])


def _ffn(x2, npre, weights, npost, layer, tm, fc):
    n, d = x2.shape
    resident = lambda w: pl.BlockSpec(w.shape, lambda i: (0, 0), pipeline_mode=pl.Buffered(1))
    return pl.pallas_call(
        functools.partial(_ffn_kernel, fc=fc),
        grid=(n // tm,),
        in_specs=[pl.BlockSpec((tm, d), lambda i: (i, 0)), _layer_spec(npre, layer, 1)]
        + [resident(w) for w in weights] + [_layer_spec(npost, layer, 1)],
        out_specs=pl.BlockSpec((tm, d), lambda i: (i, 0)),
        out_shape=jax.ShapeDtypeStruct(x2.shape, x2.dtype),
        compiler_params=pltpu.CompilerParams(dimension_semantics=("arbitrary",),
                                             vmem_limit_bytes=VMEM_LIMIT_BYTES),
        name="swiglu",
    )(x2, npre, *weights, npost)


def _pick_block(t, want):
    tb = min(want, t)
    while t % tb:
        tb //= 2
    return tb


def kernel(x, positions, norm_mix_pre, w_in, conv_w, ret_gn_w, gla_w_a2, gla_b_a, gla_gn_w, w_branch_a, w_branch_b, w_branch_c, w_out, norm_mix_post, norm_ffn_pre, w_ffn_gate, w_ffn_up, w_ffn_down, norm_ffn_post):
    b, t, d = x.shape
    depth = w_in.shape[0]
    tb = _pick_block(t, MIXER_BLOCK)
    tm = _pick_block(b * t, FFN_BLOCK)
    w_packed, cos_t, sin_t = _prep(w_in, positions)
    emat = _diag_reduce_matrix()
    rows = lambda v: v.reshape(depth, 1, -1)
    pad_rank = LANES - GLA_GATE_RANK
    wa2 = jnp.pad(gla_w_a2, ((0, 0), (0, pad_rank), (0, 0))).astype(BF16)
    mixer_params = [rows(norm_mix_pre), conv_w, rows(ret_gn_w), wa2, rows(gla_b_a), rows(gla_gn_w),
                    w_branch_a.astype(BF16), w_branch_b.astype(BF16), w_branch_c.astype(BF16),
                    w_out.astype(BF16), rows(norm_mix_post)]
    ffn_weights = [w_ffn_gate, w_ffn_up, w_ffn_down]
    for l in range(depth):
        x, *ffn_bf16 = _mixer(x, cos_t, sin_t, w_packed, mixer_params, emat, ffn_weights, l, tb)
        x = _ffn(x.reshape(b * t, d), rows(norm_ffn_pre), ffn_bf16, rows(norm_ffn_post), l, tm,
                 MXU_COLS).reshape(b, t, d)
    return x
```

```python
import functools
import math

import jax
import jax.numpy as jnp
import numpy as np
from jax import lax
from jax.experimental import pallas as pl
from jax.experimental.pallas import tpu as pltpu

D_MODEL = 1024
CONV_DIM = 512
CONV_WIDTH = 3
RET_HEADS = 4
RET_HEAD_DIM = 128
RET_DIM = RET_HEADS * RET_HEAD_DIM
GLA_HEADS = 4
GLA_KEY_HEAD_DIM = 64
GLA_VAL_HEAD_DIM = 128
GLA_KEY_DIM = GLA_HEADS * GLA_KEY_HEAD_DIM
GLA_VAL_DIM = GLA_HEADS * GLA_VAL_HEAD_DIM
GLA_GATE_RANK = 16
GLA_GATE_TAU = 16.0
N_BRANCHES = 3
D_FF = 2816
ROPE_BASE = 10000.0
EPS = 1e-6
LOG2E = 1.0 / math.log(2.0)

OFF_CONV = 0
OFF_RET = 3 * CONV_DIM
OFF_GLA = OFF_RET + 4 * RET_DIM
OFF_GA = OFF_GLA + 2 * GLA_KEY_DIM + 2 * GLA_VAL_DIM
OFF_GATES = OFF_GA + GLA_GATE_RANK
GLA_COLS = 2 * GLA_KEY_DIM + 2 * GLA_VAL_DIM

LANES = 128
W_GATES = OFF_GA
W_GA = W_GATES + N_BRANCHES * D_MODEL
SUBLANES = 8
MXU_COLS = 256
REPACK_BLOCK = 1024
W_COLS = W_GA + REPACK_BLOCK
GLA_CHUNK = 64
GLA_SUB = 8
DIAG_TILE = 16
VMEM_LIMIT_BYTES = 56 * 1024 * 1024
MIXER_BLOCK = 256
FFN_BLOCK = 1024
ROPE_BLOCK = 2048

BF16 = jnp.bfloat16
F32 = jnp.float32


def _dot(a, b):
    return jnp.dot(a, b, preferred_element_type=F32)


def _dot_nt(a, b):
    return lax.dot_general(a, b, (((1,), (1,)), ((), ())), preferred_element_type=F32)


def _dot_tn(a, b):
    return lax.dot_general(a, b, (((0,), (0,)), ((), ())), preferred_element_type=F32)


def _rms(x, w):
    return x * lax.rsqrt(jnp.mean(x * x, axis=-1, keepdims=True) + EPS) * w


def _sigmoid(x):
    return 1.0 / (1.0 + jnp.exp2(x * -LOG2E))


def _silu(x):
    return x * _sigmoid(x)


def _log_sigmoid(x):
    return jnp.minimum(x, 0.0) - jnp.log(1.0 + jnp.exp2(jnp.abs(x) * -LOG2E))


def _rope_kernel(pos_ref, invf_ref, cos_ref, sin_ref):
    tb = pos_ref.shape[1]
    hb = tb // 2
    half = RET_HEAD_DIM // 2
    pos = pos_ref[0].astype(F32)
    lo = lax.broadcasted_iota(jnp.int32, (hb, RET_HEAD_DIM), 1) < half
    ang = jnp.where(lo, pos[0:hb], pos[hb:tb]) * invf_ref[...]
    c = jnp.cos(ang)
    s = jnp.sin(ang)
    c_sw = pltpu.roll(c, half, axis=1)
    s_sw = pltpu.roll(s, half, axis=1)
    sign = jnp.where(lo, -1.0, 1.0)
    cos_ref[0, 0:hb, :] = jnp.where(lo, c, c_sw)
    cos_ref[0, hb:tb, :] = jnp.where(lo, c_sw, c)
    sin_ref[0, 0:hb, :] = jnp.where(lo, s, s_sw) * sign
    sin_ref[0, hb:tb, :] = jnp.where(lo, s_sw, s) * sign


def _rope_tables(positions, tb):
    b, t = positions.shape
    half = RET_HEAD_DIM // 2
    inv_freq = ROPE_BASE ** (-jnp.arange(half, dtype=F32) / half)
    invf = jnp.concatenate([inv_freq, inv_freq]).reshape(1, RET_HEAD_DIM)
    pos3 = positions.reshape(b, t, 1)
    out = jax.ShapeDtypeStruct((b, t, RET_HEAD_DIM), F32)
    return pl.pallas_call(
        _rope_kernel,
        grid=(b, t // tb),
        in_specs=[pl.BlockSpec((1, tb, 1), lambda i, j: (i, j, 0)),
                  pl.BlockSpec((1, RET_HEAD_DIM), lambda i, j: (0, 0))],
        out_specs=[pl.BlockSpec((1, tb, RET_HEAD_DIM), lambda i, j: (i, j, 0)),
                   pl.BlockSpec((1, tb, RET_HEAD_DIM), lambda i, j: (i, j, 0))],
        out_shape=[out, out],
        name="rope_tables",
    )(pos3, invf)


N_MAIN_BLOCKS = OFF_GA // REPACK_BLOCK
N_GATE_BLOCKS = N_BRANCHES * D_MODEL // REPACK_BLOCK


def _repack_kernel(wt_ref, out_ref):
    c = pl.program_id(1)
    w = wt_ref[0].T
    lane = lax.broadcasted_iota(jnp.int32, w.shape, 1)
    keep = (c < N_MAIN_BLOCKS + N_GATE_BLOCKS) | (lane < GLA_GATE_RANK)
    out_ref[...] = jnp.where(keep, w, 0.0).astype(BF16)


def _first_feature(c):
    gates = OFF_GATES + (c - N_MAIN_BLOCKS) * REPACK_BLOCK
    return jnp.where(c < N_MAIN_BLOCKS, c * REPACK_BLOCK,
                     jnp.where(c < N_MAIN_BLOCKS + N_GATE_BLOCKS, gates, OFF_GA))


def _prep_kernel(wt_hbm, pos_ref, invf_ref, out_ref, cos_ref, sin_ref, buf, sem, *, n_rope_units):
    n_blocks = pl.num_programs(1)
    n_steps = pl.num_programs(0) * n_blocks
    step = pl.program_id(0) * n_blocks + pl.program_id(1)

    def fetch(s):
        row0 = pl.multiple_of(_first_feature(s % n_blocks), SUBLANES)
        return pltpu.make_async_copy(wt_hbm.at[s // n_blocks, pl.ds(row0, REPACK_BLOCK), :],
                                     buf.at[s % 2], sem.at[s % 2])

    @pl.when(step == 0)
    def _():
        fetch(step).start()

    @pl.when(step + 1 < n_steps)
    def _():
        fetch(step + 1).start()

    fetch(step).wait()
    _repack_kernel(buf.at[pl.ds(step % 2, 1)], out_ref)

    @pl.when(step < n_rope_units)
    def _():
        _rope_kernel(pos_ref, invf_ref, cos_ref, sin_ref)


def _prep(w_in, positions):
    depth, d, _ = w_in.shape
    b, t = positions.shape
    w_t = jnp.swapaxes(w_in, 1, 2)
    n_blocks = N_MAIN_BLOCKS + N_GATE_BLOCKS + 1
    n_steps = depth * n_blocks

    w_in_spec = pl.BlockSpec((pl.Element(1), pl.Element(REPACK_BLOCK), pl.Element(d)),
                             lambda l, c: (l, pl.multiple_of(_first_feature(c), SUBLANES), 0))
    w_out_spec = pl.BlockSpec((None, d, REPACK_BLOCK), lambda l, c: (l, 0, c))
    w_out_shape = jax.ShapeDtypeStruct((depth, d, W_COLS), BF16)
    params = pltpu.CompilerParams(dimension_semantics=("arbitrary", "arbitrary"), vmem_limit_bytes=VMEM_LIMIT_BYTES)

    k = 1
    while b * k * 2 <= n_steps and t % (k * 2 * DIAG_TILE) == 0:
        k *= 2
    if b * k > n_steps:
        w_packed = pl.pallas_call(_repack_kernel, grid=(depth, n_blocks), in_specs=[w_in_spec],
                                  out_specs=w_out_spec, out_shape=w_out_shape, compiler_params=params,
                                  name="repack_w_in")(w_t)
        return (w_packed,) + tuple(_rope_tables(positions, _pick_block(t, ROPE_BLOCK)))

    rb = t // k
    n_units = b * k

    def unit(l, c):
        u = jnp.minimum(l * n_blocks + c, n_units - 1)
        return (u // k, u % k, 0)

    half = RET_HEAD_DIM // 2
    inv_freq = ROPE_BASE ** (-jnp.arange(half, dtype=F32) / half)
    invf = jnp.concatenate([inv_freq, inv_freq]).reshape(1, RET_HEAD_DIM)
    table = jax.ShapeDtypeStruct((b, t, RET_HEAD_DIM), F32)
    return pl.pallas_call(
        functools.partial(_prep_kernel, n_rope_units=n_units),
        grid=(depth, n_blocks),
        in_specs=[pl.BlockSpec(memory_space=pl.ANY), pl.BlockSpec((1, rb, 1), unit),
                  pl.BlockSpec((1, RET_HEAD_DIM), lambda l, c: (0, 0))],
        out_specs=[w_out_spec, pl.BlockSpec((1, rb, RET_HEAD_DIM), unit), pl.BlockSpec((1, rb, RET_HEAD_DIM), unit)],
        out_shape=[w_out_shape, table, table],
        scratch_shapes=[pltpu.VMEM((2, REPACK_BLOCK, d), F32), pltpu.SemaphoreType.DMA((2,))],
        compiler_params=params,
        name="prep",
    )(w_t, positions.reshape(b, t, 1), invf)


def _mixer_kernel(x_ref, xnext_ref, cos_ref, sin_ref, win_ref, npre_ref, convw_ref,
                  retgn_ref, wa2_ref, ba_ref, glagn_ref, wba_ref, wbb_ref, wbc_ref, wout_ref,
                  npost_ref, emat_ref, wg32_ref, wu32_ref, wd32_ref, out_ref, wg16_ref, wu16_ref, wd16_ref,
                  h_s, hn_s, pa_s, pb_s, pc_s, gate_s, ob_s, oc_s, m_s, zc_s, rstate_s, gstate_s, cum_s,
                  td_s, srep_s, dmask_s, rdec_s, *, tb):
    b_idx = pl.program_id(0)
    t_idx = pl.program_id(1)

    @pl.when((b_idx == 0) & (t_idx == 0))
    def _():
        ri = lax.broadcasted_iota(jnp.int32, (tb, tb), 0)
        ci = lax.broadcasted_iota(jnp.int32, (tb, tb), 1)
        dif = jnp.maximum(ri - ci, 0).astype(F32)
        rowf = lax.broadcasted_iota(jnp.int32, (tb, RET_HEAD_DIM), 0).astype(F32)
        for hh in range(RET_HEADS):
            log_g = math.log(1.0 - 2.0 ** (-5.0 - hh))
            dmask_s[hh] = jnp.where(ri >= ci, jnp.exp(log_g * dif), 0.0)
            rdec_s[2 * hh] = jnp.exp(log_g * (rowf + 1.0))
            rdec_s[2 * hh + 1] = jnp.exp(log_g * (tb - 1.0 - rowf))

    @pl.when(t_idx == 0)
    def _():
        zc_s[0:SUBLANES, :] = jnp.zeros((SUBLANES, CONV_DIM), F32)
        rstate_s[...] = jnp.zeros(rstate_s.shape, F32)
        gstate_s[...] = jnp.zeros(gstate_s.shape, F32)

    def proj_task(dst_ref, w_ref, w_lo, lo, act=None, h_ref=h_s):
        def run():
            r = _dot(h_ref[...], w_ref[:, w_lo + lo:w_lo + lo + MXU_COLS])
            dst_ref[:, lo:lo + MXU_COLS] = r if act is None else act(r)
        return run

    ret_q_cols = range(0, RET_DIM + MXU_COLS, MXU_COLS)

    @pl.when((b_idx == 0) & (t_idx == 0))
    def _():
        h_s[...] = _rms(x_ref[0], npre_ref[...]).astype(BF16)
        for lo in ret_q_cols:
            proj_task(pb_s, win_ref, OFF_RET, lo)()

    half_tb = tb // 2

    def next_h_task(r0):
        def run():
            xn = xnext_ref[0, r0:r0 + half_tb, :]
            hn_s[r0:r0 + half_tb, :] = _rms(xn, npre_ref[...]).astype(BF16)
        return run

    def ffn_cast_task():
        wg16_ref[...] = wg32_ref[...].astype(BF16)
        wu16_ref[...] = wu32_ref[...].astype(BF16)
        wd16_ref[...] = wd32_ref[...].astype(BF16)

    def interleave(primary, fillers):
        done = 0
        for i, step in enumerate(primary):
            step()
            while done * len(primary) < (i + 1) * len(fillers):
                fillers[done]()
                done += 1

    ret_proj = [proj_task(pb_s, win_ref, OFF_RET, lo)
                for lo in range(RET_DIM + MXU_COLS, 4 * RET_DIM, MXU_COLS)]
    conv_proj = [proj_task(pa_s, win_ref, OFF_CONV, lo) for lo in range(0, 3 * CONV_DIM, MXU_COLS)]
    gate_proj = [proj_task(gate_s, win_ref, W_GATES, lo, _sigmoid) for lo in range(0, N_BRANCHES * D_MODEL, MXU_COLS)]

    gc = GLA_CHUNK
    kd = GLA_KEY_DIM
    qscale = GLA_KEY_HEAD_DIM ** -0.5
    n_chunks = tb // gc
    gla_proj = [proj_task(pc_s, win_ref, OFF_GLA, lo) for lo in range(0, GLA_COLS, MXU_COLS)]
    ga_down = _dot(h_s[...], win_ref[:, W_GA:W_GA + LANES])
    logits = _dot(ga_down.astype(BF16), wa2_ref[...]) + ba_ref[...]
    for task in gla_proj[:3]:
        task()
    log_a = _log_sigmoid(logits) * (LOG2E / GLA_GATE_TAU)
    ri = lax.broadcasted_iota(jnp.int32, (gc, gc), 0)
    ci = lax.broadcasted_iota(jnp.int32, (gc, gc), 1)
    tril = jnp.where(ri >= ci, 1.0, 0.0).astype(BF16)
    for c in range(n_chunks):
        g = log_a[c * gc:(c + 1) * gc, :]
        g_hi = g.astype(BF16)
        g_lo = (g - g_hi.astype(F32)).astype(BF16)
        cum_s[c * gc:(c + 1) * gc, :] = _dot(tril, g_hi) + _dot(tril, g_lo)
    for task in gla_proj[3:]:
        task()

    def diag_unit(u):
        r0 = u * DIAG_TILE
        subs = []
        for rs in range(r0, r0 + DIAG_TILE, GLA_SUB):
            subs.append((rs, pc_s[rs:rs + GLA_SUB, 0:kd] * qscale, cum_s[rs:rs + GLA_SUB, :]))
        for j in range(GLA_SUB):
            tjs = []
            for rs, qb, cb in subs:
                kj = pc_s[rs + j:rs + j + 1, kd:2 * kd]
                cj = cum_s[rs + j:rs + j + 1, :]
                tjs.append(qb * kj * jnp.exp2(jnp.minimum(cb - cj, 0.0)))
            td_s[r0:r0 + DIAG_TILE, j * kd:(j + 1) * kd] = jnp.concatenate(tjs, axis=0).astype(BF16)

    interleave([functools.partial(diag_unit, u) for u in range(tb // DIAG_TILE)], ret_proj + conv_proj[:4])

    def conv_task():
        z = pa_s[:, 2 * CONV_DIM:3 * CONV_DIM] * pa_s[:, 0:CONV_DIM]
        zc_s[SUBLANES:SUBLANES + tb, :] = z
        z1 = zc_s[SUBLANES - 1:SUBLANES - 1 + tb, :]
        z2 = zc_s[SUBLANES - 2:SUBLANES - 2 + tb, :]
        cw = convw_ref[...]
        y_a = pa_s[:, CONV_DIM:2 * CONV_DIM] * (cw[0:1, :] * z2 + cw[1:2, :] * z1 + cw[2:3, :] * z)
        zc_s[0:SUBLANES, :] = z[tb - SUBLANES:tb, :]
        m_s[...] = gate_s[:, 0:D_MODEL] * _dot(y_a.astype(BF16), wba_ref[...])

    ret = [dict() for _ in range(RET_HEADS)]

    def ret_rotary(hh):
        lo = hh * RET_HEAD_DIM
        cosf = cos_ref[0]
        sins = sin_ref[0]
        qh = pb_s[:, lo:lo + RET_HEAD_DIM]
        kh = pb_s[:, RET_DIM + lo:RET_DIM + lo + RET_HEAD_DIM]
        qc = (qh * cosf + pltpu.roll(qh, RET_HEAD_DIM // 2, axis=1) * sins) * (RET_HEAD_DIM ** -0.5)
        kc = kh * cosf + pltpu.roll(kh, RET_HEAD_DIM // 2, axis=1) * sins
        ret[hh].update(q=qc.astype(BF16), k=kc.astype(BF16),
                       qd=(qc * rdec_s[2 * hh]).astype(BF16), kd=(kc * rdec_s[2 * hh + 1]).astype(BF16),
                       v=pb_s[:, 2 * RET_DIM + lo:2 * RET_DIM + lo + RET_HEAD_DIM].astype(BF16))

    def ret_scores(hh):
        r = ret[hh]
        log_g = math.log(1.0 - 2.0 ** (-5.0 - hh))
        state = rstate_s[hh]
        r["s"] = (_dot_nt(r["q"], r["k"]) * dmask_s[hh]).astype(BF16)
        r["inter"] = _dot(r["qd"], state.astype(BF16))
        rstate_s[hh] = state * math.exp(log_g * tb) + _dot_tn(r["kd"], r["v"])

    def ret_output(hh):
        r = ret[hh]
        lo = hh * RET_HEAD_DIM
        o = _dot(r["s"], r["v"]) + r["inter"]
        oc = o - jnp.mean(o, axis=-1, keepdims=True)
        ob_s[:, lo:lo + RET_HEAD_DIM] = oc * lax.rsqrt(jnp.mean(oc * oc, axis=-1, keepdims=True) + EPS)

    rowi = lax.broadcasted_iota(jnp.int32, (gc, LANES), 0)
    lanei = lax.broadcasted_iota(jnp.int32, (gc, LANES), 1)
    colj = lanei & (GLA_KEY_HEAD_DIM - 1)
    sub_shift = GLA_SUB.bit_length() - 1
    diag_mask = ((colj >> sub_shift) == (rowi >> sub_shift)) & (colj <= rowi)
    head_lo = lanei < GLA_KEY_HEAD_DIM
    srow = lax.broadcasted_iota(jnp.int32, (2 * GLA_VAL_HEAD_DIM, LANES), 0)
    slane = lax.broadcasted_iota(jnp.int32, (2 * GLA_VAL_HEAD_DIM, LANES), 1)
    state_mask = (srow < GLA_VAL_HEAD_DIM) == (slane < GLA_KEY_HEAD_DIM)
    zero_v = jnp.zeros((gc, GLA_VAL_HEAD_DIM), BF16)
    block_masks = {}
    half = gc // 4
    while half >= GLA_SUB:
        shift = (2 * half).bit_length() - 1
        block_masks[half] = (colj >> shift) == (rowi >> shift)
        half //= 2

    gla = [[dict() for _ in range(GLA_HEADS // 2)] for _ in range(n_chunks)]

    def gla_intra(c):
        rows = slice(c * gc, (c + 1) * gc)
        cum = cum_s[rows, :]
        last = cum[gc - 1:gc, :]
        q = pc_s[rows, 0:kd] * qscale
        k = pc_s[rows, kd:2 * kd]
        q_in = q * jnp.exp2(cum)
        k_st = k * jnp.exp2(last - cum)
        levels = []
        half = gc // 2
        while half >= GLA_SUB:
            a_rows, b_rows = [], []
            zero_half = jnp.zeros((half, kd), F32)
            for base in range(0, gc, 2 * half):
                mid = base + half
                ref = cum[mid - 1:mid, :]
                b_rows += [k[base:mid, :] * jnp.exp2(ref - cum[base:mid, :]), zero_half]
                a_rows += [zero_half, q[mid:mid + half, :] * jnp.exp2(cum[mid:mid + half, :] - ref)]
            levels.append((half, jnp.concatenate(a_rows, axis=0), jnp.concatenate(b_rows, axis=0)))
            half //= 2
        for pr in range(GLA_HEADS // 2):
            ls = slice(pr * LANES, (pr + 1) * LANES)
            s_off = None
            for half, a, b in levels:
                b_two = jnp.concatenate([jnp.where(head_lo, b[:, ls], 0.0).astype(BF16),
                                         jnp.where(head_lo, 0.0, b[:, ls]).astype(BF16)], axis=0)
                r = _dot_nt(a[:, ls].astype(BF16), b_two)
                if 2 * half < gc:
                    r = jnp.where(block_masks[half], r, 0.0)
                s_off = r if s_off is None else s_off + r
            s = jnp.where(diag_mask, srep_s[rows, ls], s_off).astype(BF16)
            v_lo = 2 * kd + 2 * pr * GLA_VAL_HEAD_DIM
            v0 = pc_s[rows, v_lo:v_lo + GLA_VAL_HEAD_DIM].astype(BF16)
            v1 = pc_s[rows, v_lo + GLA_VAL_HEAD_DIM:v_lo + 2 * GLA_VAL_HEAD_DIM].astype(BF16)
            v_bd = jnp.concatenate([jnp.concatenate([v0, zero_v], axis=1),
                                    jnp.concatenate([zero_v, v1], axis=1)], axis=0)
            v_pair = jnp.concatenate([v0, v1], axis=1)
            upd = _dot_tn(v_pair, k_st[:, ls].astype(BF16))
            gla[c][pr].update(intra=_dot(s, v_bd), q_in=q_in[:, ls].astype(BF16),
                              upd=jnp.where(state_mask, upd, 0.0), decay=jnp.exp2(last[:, ls]))

    def gla_carry():
        for pr in range(GLA_HEADS // 2):
            st = gstate_s[pr]
            for c in range(n_chunks):
                g = gla[c][pr]
                o = g["intra"] + _dot_nt(g["q_in"], st.astype(BF16))
                st = st * g["decay"] + g["upd"]
                for hl in range(2):
                    oh = o[:, hl * GLA_VAL_HEAD_DIM:(hl + 1) * GLA_VAL_HEAD_DIM]
                    lo = (2 * pr + hl) * GLA_VAL_HEAD_DIM
                    oc_s[c * gc:(c + 1) * gc, lo:lo + GLA_VAL_HEAD_DIM] = (
                        oh * lax.rsqrt(jnp.mean(oh * oh, axis=-1, keepdims=True) + EPS))
            gstate_s[pr] = st

    def srep_task():
        srep_s[...] = _dot(td_s[...], emat_ref[...])

    heads = range(RET_HEADS)
    interleave([next_h_task(0), next_h_task(half_tb), ffn_cast_task]
               + [functools.partial(ret_rotary, hh) for hh in heads], conv_proj[4:])
    interleave([srep_task] + [functools.partial(ret_scores, hh) for hh in heads], gate_proj[:4])
    interleave([functools.partial(gla_intra, c) for c in range(n_chunks)], gate_proj[4:8])
    interleave([functools.partial(ret_output, hh) for hh in heads], gate_proj[8:])
    h_s[...] = hn_s[...]
    interleave([gla_carry], [conv_task])

    y_b = ob_s[...] * retgn_ref[...] * _silu(pb_s[:, 3 * RET_DIM:4 * RET_DIM])
    m_s[...] += gate_s[:, D_MODEL:2 * D_MODEL] * _dot(y_b.astype(BF16), wbb_ref[...])
    y_c = oc_s[...] * glagn_ref[...] * _silu(pc_s[:, 2 * kd + GLA_VAL_DIM:2 * kd + 2 * GLA_VAL_DIM])
    merged = m_s[...] + gate_s[:, 2 * D_MODEL:3 * D_MODEL] * _dot(y_c.astype(BF16), wbc_ref[...])

    merged_b = merged.astype(BF16)
    sumsq = jnp.zeros((tb, 1), F32)
    for lo in range(0, D_MODEL, MXU_COLS):
        blk = _dot(merged_b, wout_ref[:, lo:lo + MXU_COLS])
        m_s[:, lo:lo + MXU_COLS] = blk
        sumsq = sumsq + jnp.sum(blk * blk, axis=-1, keepdims=True)
    for lo in ret_q_cols:
        proj_task(pb_s, win_ref, OFF_RET, lo, h_ref=hn_s)()
    scale = lax.rsqrt(sumsq * (1.0 / D_MODEL) + EPS)
    out_ref[0] = x_ref[0] + m_s[...] * scale * npost_ref[...]


def _layer_spec(arr, layer, grid_rank):
    index = (lambda i: (layer, 0, 0)) if grid_rank == 1 else (lambda i, j: (layer, 0, 0))
    return pl.BlockSpec((None,) + arr.shape[1:], index, pipeline_mode=pl.Buffered(1))


def _diag_reduce_matrix():
    kidx = np.arange(GLA_SUB * GLA_KEY_DIM)
    j = kidx // GLA_KEY_DIM
    hk = (kidx % GLA_KEY_DIM) // GLA_KEY_HEAD_DIM
    col = np.arange(GLA_KEY_DIM)
    hc = col // GLA_KEY_HEAD_DIM
    cj = (col % GLA_KEY_HEAD_DIM) % GLA_SUB
    return jnp.asarray((hk[:, None] == hc[None, :]) & (j[:, None] == cj[None, :]), dtype=BF16)


def _slab_spec(arr, layer, n_steps, nt, stacked):
    rows, cols = arr.shape[-2:]
    n_slabs = 1
    while n_slabs * 2 <= n_steps and rows % (n_slabs * 2 * DIAG_TILE) == 0:
        n_slabs *= 2
    per = n_steps // n_slabs
    slab = lambda i, j: jnp.minimum((i * nt + j) // per, n_slabs - 1)
    if stacked:
        return pl.BlockSpec((None, rows // n_slabs, cols), lambda i, j: (layer, slab(i, j), 0))
    return pl.BlockSpec((rows // n_slabs, cols), lambda i, j: (slab(i, j), 0))


def _mixer(x, cos_t, sin_t, w_packed, layer_params, emat, ffn_weights, layer, tb):
    b, t, d = x.shape
    nt = t // tb
    n_steps = b * nt
    tok = lambda width: pl.BlockSpec((1, tb, width), lambda i, j: (i, j, 0))
    next_tok = pl.BlockSpec((1, tb, d), lambda i, j: (jnp.minimum(i + (j + 1) // nt, b - 1), (j + 1) % nt, 0))
    const_specs = [pl.BlockSpec((None, d, W_GA + LANES), lambda i, j: (layer, 0, 0), pipeline_mode=pl.Buffered(1))]
    const_specs += [_layer_spec(p, layer, 2) for p in layer_params]
    const_specs.append(pl.BlockSpec(emat.shape, lambda i, j: (0, 0), pipeline_mode=pl.Buffered(1)))
    scratch = [
        pltpu.VMEM((tb, D_MODEL), BF16),
        pltpu.VMEM((tb, D_MODEL), BF16),
        pltpu.VMEM((tb, 3 * CONV_DIM), F32),
        pltpu.VMEM((tb, 4 * RET_DIM), F32),
        pltpu.VMEM((tb, GLA_COLS), F32),
        pltpu.VMEM((tb, N_BRANCHES * D_MODEL), F32),
        pltpu.VMEM((tb, RET_DIM), F32),
        pltpu.VMEM((tb, GLA_VAL_DIM), F32),
        pltpu.VMEM((tb, D_MODEL), F32),
        pltpu.VMEM((tb + SUBLANES, CONV_DIM), F32),
        pltpu.VMEM((RET_HEADS, RET_HEAD_DIM, RET_HEAD_DIM), F32),
        pltpu.VMEM((GLA_HEADS // 2, 2 * GLA_VAL_HEAD_DIM, LANES), F32),
        pltpu.VMEM((tb, GLA_KEY_DIM), F32),
        pltpu.VMEM((tb, GLA_SUB * GLA_KEY_DIM), BF16),
        pltpu.VMEM((tb, GLA_KEY_DIM), F32),
        pltpu.VMEM((RET_HEADS, tb, tb), F32),
        pltpu.VMEM((2 * RET_HEADS, tb, RET_HEAD_DIM), F32),
    ]
    return pl.pallas_call(
        functools.partial(_mixer_kernel, tb=tb),
        grid=(b, t // tb),
        in_specs=[tok(d), next_tok, tok(RET_HEAD_DIM), tok(RET_HEAD_DIM)] + const_specs
        + [_slab_spec(w, layer, n_steps, nt, True) for w in ffn_weights],
        out_specs=[tok(d)] + [_slab_spec(w, layer, n_steps, nt, False) for w in ffn_weights],
        out_shape=[jax.ShapeDtypeStruct(x.shape, x.dtype)]
        + [jax.ShapeDtypeStruct(w.shape[1:], BF16) for w in ffn_weights],
        scratch_shapes=scratch,
        compiler_params=pltpu.CompilerParams(dimension_semantics=("arbitrary", "arbitrary"),
                                             vmem_limit_bytes=VMEM_LIMIT_BYTES),
        name="mixer",
    )(x, x, cos_t, sin_t, w_packed, *layer_params, emat, *ffn_weights)


def _ffn_kernel(x_ref, npre_ref, wg_ref, wu_ref, wd_ref, npost_ref, out_ref, *, fc):
    tm = x_ref.shape[0]
    hm = tm // 2
    halves = (slice(0, hm), slice(hm, tm))
    h_halves = [_rms(x_ref[rows, :], npre_ref[...]).astype(BF16) for rows in halves]
    chunks = list(range(0, D_FF, fc))
    assert len(chunks) >= 2 and D_FF % fc == 0

    def hidden(h, f):
        g = _dot(h, wg_ref[:, f:f + fc])
        u = _dot(h, wu_ref[:, f:f + fc])
        return (_silu(g) * u).astype(BF16)

    f0 = chunks[0]
    acc = jnp.concatenate([_dot(hidden(h, f0), wd_ref[f0:f0 + fc, :]) for h in h_halves], axis=0)
    h = jnp.concatenate(h_halves, axis=0)
    for f in chunks[1:-1]:
        acc = acc + _dot(hidden(h, f), wd_ref[f:f + fc, :])
    f1 = chunks[-1]
    for rows, h_half in zip(halves, h_halves):
        total = acc[rows, :] + _dot(hidden(h_half, f1), wd_ref[f1:f1 + fc, :])
        out_ref[rows, :] = x_ref[rows, :] + _rms(total, npost_ref[...])


def _ffn(x2, npre, weights, npost, layer, tm, fc):
    n, d = x2.shape
    resident = lambda w: pl.BlockSpec(w.shape, lambda i: (0, 0), pipeline_mode=pl.Buffered(1))
    return pl.pallas_call(
        functools.partial(_ffn_kernel, fc=fc),
        grid=(n // tm,),
        in_specs=[pl.BlockSpec((tm, d), lambda i: (i, 0)), _layer_spec(npre, layer, 1)]
        + [resident(w) for w in weights] + [_layer_spec(npost, layer, 1)],
        out_specs=pl.BlockSpec((tm, d), lambda i: (i, 0)),
        out_shape=jax.ShapeDtypeStruct(x2.shape, x2.dtype),
        compiler_params=pltpu.CompilerParams(dimension_semantics=("arbitrary",),
                                             vmem_limit_bytes=VMEM_LIMIT_BYTES),
        name="swiglu",
    )(x2, npre, *weights, npost)


def _pick_block(t, want):
    tb = min(want, t)
    while t % tb:
        tb //= 2
    return tb


def kernel(x, positions, norm_mix_pre, w_in, conv_w, ret_gn_w, gla_w_a2, gla_b_a, gla_gn_w, w_branch_a, w_branch_b, w_branch_c, w_out, norm_mix_post, norm_ffn_pre, w_ffn_gate, w_ffn_up, w_ffn_down, norm_ffn_post):
    b, t, d = x.shape
    depth = w_in.shape[0]
    tb = _pick_block(t, MIXER_BLOCK)
    tm = _pick_block(b * t, FFN_BLOCK)
    w_packed, cos_t, sin_t = _prep(w_in, positions)
    emat = _diag_reduce_matrix()
    rows = lambda v: v.reshape(depth, 1, -1)
    pad_rank = LANES - GLA_GATE_RANK
    wa2 = jnp.pad(gla_w_a2, ((0, 0), (0, pad_rank), (0, 0))).astype(BF16)
    mixer_params = [rows(norm_mix_pre), conv_w, rows(ret_gn_w), wa2, rows(gla_b_a), rows(gla_gn_w),
                    w_branch_a.astype(BF16), w_branch_b.astype(BF16), w_branch_c.astype(BF16),
                    w_out.astype(BF16), rows(norm_mix_post)]
    ffn_weights = [w_ffn_gate, w_ffn_up, w_ffn_down]
    for l in range(depth):
        x, *ffn_bf16 = _mixer(x, cos_t, sin_t, w_packed, mixer_params, emat, ffn_weights, l, tb)
        x = _ffn(x.reshape(b * t, d), rows(norm_ffn_pre), ffn_bf16, rows(norm_ffn_post), l, tm,
                 MXU_COLS).reshape(b, t, d)
    return x
```
